```python
import math
import jax, jax.numpy as jnp
from jax import lax
import numpy as np

D_MODEL = 1024
BATCH = 2
SEQ = 8192
DEPTH = 1

N_MEM = 256
D_LRU = 512
LRU_BLOCKS = 8
CONV_W = 4
LRU_C = 8.0
N_HEADS = 8
N_KV = 2
HEAD_DIM = 64
D_NSA = N_HEADS * HEAD_DIM
D_MIX = D_LRU + D_NSA
CMP_BLOCK = 32
CMP_STRIDE = 16
SEL_BLOCK = 64
N_SELECT = 16
WINDOW = 512
Q_BLOCK = 128
MEM_HEADS = 4
MEM_HEAD_DIM = D_MODEL // MEM_HEADS
PEER_HEADS = 8
PEER_NKEYS = 128
PEER_N = PEER_NKEYS * PEER_NKEYS
PEER_TOPK = 16
PEER_QDIM = 256
PEER_TOK_BLOCK = 128
EPS = 1e-6
KV_W = N_KV * HEAD_DIM
IN_SIZES = [D_LRU, D_LRU, D_NSA, KV_W, KV_W, KV_W, KV_W, KV_W, KV_W, 3 * N_HEADS]
N_IN = sum(IN_SIZES)

kernel_name = 'hymba_style_rglru_nsa_peer_block'


def rmsnorm(x, g):
    xf = x.astype(jnp.float32)
    y = xf * lax.rsqrt(jnp.mean(xf * xf, axis=-1, keepdims=True) + EPS)
    return (y * g.astype(jnp.float32)).astype(x.dtype)


def masked_softmax(s, mask):
    s = jnp.where(mask, s.astype(jnp.float32), -1e30)
    p = jax.nn.softmax(s, axis=-1)
    return jnp.where(mask, p, 0.0)


def rg_lru(xb, gate, conv_w, conv_b, w_a, b_a, w_i, b_i, lam):
    B, S, _ = xb.shape
    xp = jnp.pad(xb, ((0, 0), (CONV_W - 1, 0), (0, 0)))
    xc = conv_b
    for k in range(CONV_W):
        xc = xc + xp[:, k:k + S] * conv_w[k]
    xh = xc.reshape(B, S, LRU_BLOCKS, D_LRU // LRU_BLOCKS)
    r = jax.nn.sigmoid(jnp.einsum('bshi,hij->bshj', xh, w_a) + b_a).reshape(B, S, D_LRU)
    i = jax.nn.sigmoid(jnp.einsum('bshi,hij->bshj', xh, w_i) + b_i).reshape(B, S, D_LRU)
    log_a = -LRU_C * r.astype(jnp.float32) * jax.nn.softplus(-lam.astype(jnp.float32))
    a = jnp.exp(log_a)
    b = jnp.sqrt(-jnp.expm1(2.0 * log_a)) * (i * xc).astype(jnp.float32)

    def comb(left, right):
        a1, b1 = left
        a2, b2 = right
        return a1 * a2, a2 * b1 + b2

    _, h = lax.associative_scan(comb, (a, b), axis=1)
    return h.astype(xb.dtype) * jax.nn.gelu(gate)


def compress(k_raw, pos, w1, w2):
    B, S = k_raw.shape[:2]
    nc = (S - CMP_BLOCK) // CMP_STRIDE + 1
    idx = jnp.arange(nc)[:, None] * CMP_STRIDE + jnp.arange(CMP_BLOCK)[None, :]
    blk = k_raw[:, idx] + pos[None, None, :, None, :]
    blk = blk.transpose(0, 1, 3, 2, 4).reshape(B, nc, N_KV, CMP_BLOCK * HEAD_DIM)
    return jax.nn.gelu(blk @ w1) @ w2


def nsa(q, kc, vc, ks, vs, kw, vw, gates):
    B, S = q.shape[:2]
    G = N_HEADS // N_KV
    nc = kc.shape[1]
    nsb = S // SEL_BLOCK
    nsel = min(N_SELECT, nsb)
    ci = jnp.arange(nc)[:, None] * CMP_STRIDE
    bj = jnp.arange(nsb)[None, :] * SEL_BLOCK
    overlap = ((ci < bj + SEL_BLOCK) & (ci + CMP_BLOCK > bj)).astype(jnp.float32)
    cmp_end = jnp.arange(nc) * CMP_STRIDE + CMP_BLOCK - 1
    ks_b = ks.reshape(B, nsb, SEL_BLOCK, N_KV, HEAD_DIM).transpose(0, 3, 1, 2, 4)
    vs_b = vs.reshape(B, nsb, SEL_BLOCK, N_KV, HEAD_DIM).transpose(0, 3, 1, 2, 4)
    kw_p = jnp.pad(kw, ((0, 0), (WINDOW, 0), (0, 0), (0, 0)))
    vw_p = jnp.pad(vw, ((0, 0), (WINDOW, 0), (0, 0), (0, 0)))
    q5 = q.reshape(B, S, N_KV, G, HEAD_DIM) * (HEAD_DIM ** -0.5)
    bi = jnp.arange(B)[:, None, None, None]
    hi = jnp.arange(N_KV)[None, :, None, None]
    jb = jnp.arange(nsb)

    def block(c):
        s0 = c * Q_BLOCK
        t = s0 + jnp.arange(Q_BLOCK)
        qc = lax.dynamic_slice_in_dim(q5, s0, Q_BLOCK, axis=1)
        gc = lax.dynamic_slice_in_dim(gates, s0, Q_BLOCK, axis=1)
        sc = jnp.einsum('bqkgd,bnkd->bkgqn', qc, kc)
        mc = cmp_end[None, :] <= t[:, None]
        pc = masked_softmax(sc, mc)
        o_cmp = jnp.einsum('bkgqn,bnkd->bqkgd', pc.astype(vc.dtype), vc)
        imp = jnp.einsum('bkgqn,nj->bkqj', pc, overlap)
        cur = t // SEL_BLOCK
        valid = jb[None, :] <= cur[:, None]
        forced = (jb[None, :] == 0) | (jb[None, :] == cur[:, None]) | (jb[None, :] == cur[:, None] - 1)
        imp = jnp.where(valid, jnp.where(forced, jnp.inf, imp), -jnp.inf)
        top, idx = lax.top_k(imp, nsel)
        kg = ks_b[bi, hi, idx]
        vg = vs_b[bi, hi, idx]
        tok = idx[..., None] * SEL_BLOCK + jnp.arange(SEL_BLOCK)
        ms = (top > -1.0)[..., None] & (tok <= t[None, None, :, None, None])
        ss = jnp.einsum('bqkgd,bkqsld->bkgqsl', qc, kg)
        ps = masked_softmax(ss.reshape(B, N_KV, G, Q_BLOCK, nsel * SEL_BLOCK),
                            ms.reshape(B, N_KV, 1, Q_BLOCK, nsel * SEL_BLOCK)).reshape(ss.shape)
        o_slc = jnp.einsum('bkgqsl,bkqsld->bqkgd', ps.astype(vg.dtype), vg)
        kwc = lax.dynamic_slice_in_dim(kw_p, s0, Q_BLOCK + WINDOW, axis=1)
        vwc = lax.dynamic_slice_in_dim(vw_p, s0, Q_BLOCK + WINDOW, axis=1)
        kpos = s0 - WINDOW + jnp.arange(Q_BLOCK + WINDOW)
        mw = (kpos[None, :] <= t[:, None]) & (kpos[None, :] > t[:, None] - WINDOW) & (kpos[None, :] >= 0)
        sw = jnp.einsum('bqkgd,bnkd->bkgqn', qc, kwc)
        pw = masked_softmax(sw, mw)
        o_win = jnp.einsum('bkgqn,bnkd->bqkgd', pw.astype(vwc.dtype), vwc)
        out = gc[..., 0:1] * o_cmp + gc[..., 1:2] * o_slc + gc[..., 2:3] * o_win
        return out.reshape(B, Q_BLOCK, D_NSA)

    outs = lax.map(block, jnp.arange(S // Q_BLOCK))
    return outs.transpose(1, 0, 2, 3).reshape(B, S, D_NSA)


def mem_attn(xn, memn, wq, wk, wv, wo):
    B, S, _ = xn.shape
    M = memn.shape[1]
    q = (xn @ wq).reshape(B, S, MEM_HEADS, MEM_HEAD_DIM) * (MEM_HEAD_DIM ** -0.5)
    k = (memn @ wk).reshape(B, M, MEM_HEADS, MEM_HEAD_DIM)
    v = (memn @ wv).reshape(B, M, MEM_HEADS, MEM_HEAD_DIM)
    s = jnp.einsum('bshd,bmhd->bhsm', q, k).astype(jnp.float32)
    p = jax.nn.softmax(s, axis=-1).astype(v.dtype)
    o = jnp.einsum('bhsm,bmhd->bshd', p, v).reshape(B, S, D_MODEL)
    return o @ wo


def peer(xn, w_pq, sub_keys, U, V):
    B, S, D = xn.shape
    xt = xn.reshape(-1, PEER_TOK_BLOCK, D)

    def block(xc):
        T = xc.shape[0]
        q = (xc @ w_pq).reshape(T, PEER_HEADS, 2, PEER_QDIM // 2)
        s = jnp.einsum('thcd,hcnd->thcn', q, sub_keys).astype(jnp.float32)
        v1, i1 = lax.top_k(s[:, :, 0], PEER_TOPK)
        v2, i2 = lax.top_k(s[:, :, 1], PEER_TOPK)
        cand = (v1[..., :, None] + v2[..., None, :]).reshape(T, PEER_HEADS, PEER_TOPK * PEER_TOPK)
        cid = (i1[..., :, None] * PEER_NKEYS + i2[..., None, :]).reshape(T, PEER_HEADS, PEER_TOPK * PEER_TOPK)
        top, pos = lax.top_k(cand, PEER_TOPK)
        eid = jnp.take_along_axis(cid, pos, axis=-1)
        g = jax.nn.softmax(top, axis=-1).astype(xc.dtype)
        u = U[eid]
        v = V[eid]
        h = jax.nn.gelu(jnp.einsum('td,thkd->thk', xc, u))
        return jnp.einsum('thk,thkd->td', g * h, v)

    return lax.map(block, xt).reshape(B, S, D)


def setup_inputs(seed: int = 0) -> dict:
    key = jax.random.key(seed)
    ks = iter(jax.random.split(key, 40))
    f32 = jnp.float32

    def nrm(shape, scale):
        return jax.random.normal(next(ks), shape, f32) * scale

    def gain(shape):
        return 1.0 + 0.02 * jax.random.normal(next(ks), shape, f32)

    L = DEPTH
    bd = D_LRU // LRU_BLOCKS
    u = jax.random.uniform(next(ks), (L, D_LRU), f32, minval=0.9, maxval=0.999)
    a0 = u ** (1.0 / LRU_C)
    lam = jnp.log(a0) - jnp.log1p(-a0)
    return {
        'x': nrm((BATCH, SEQ, D_MODEL), 1.0),
        'mem': nrm((BATCH, N_MEM, D_MODEL), 1.0),
        'g_mix': gain((L, D_MODEL)),
        'w_in': nrm((L, D_MODEL, N_IN), D_MODEL ** -0.5),
        'b_gate': nrm((L, 3 * N_HEADS), 0.1),
        'conv_w': nrm((L, CONV_W, D_LRU), CONV_W ** -0.5),
        'conv_b': nrm((L, D_LRU), 0.02),
        'w_rg_a': nrm((L, LRU_BLOCKS, bd, bd), bd ** -0.5),
        'b_rg_a': nrm((L, LRU_BLOCKS, bd), 0.1),
        'w_rg_i': nrm((L, LRU_BLOCKS, bd, bd), bd ** -0.5),
        'b_rg_i': nrm((L, LRU_BLOCKS, bd), 0.1),
        'lam': lam,
        'cmp_pos_k': nrm((L, CMP_BLOCK, HEAD_DIM), 0.02),
        'cmp_w1_k': nrm((L, CMP_BLOCK * HEAD_DIM, HEAD_DIM), (CMP_BLOCK * HEAD_DIM) ** -0.5),
        'cmp_w2_k': nrm((L, HEAD_DIM, HEAD_DIM), HEAD_DIM ** -0.5),
        'cmp_pos_v': nrm((L, CMP_BLOCK, HEAD_DIM), 0.02),
        'cmp_w1_v': nrm((L, CMP_BLOCK * HEAD_DIM, HEAD_DIM), (CMP_BLOCK * HEAD_DIM) ** -0.5),
        'cmp_w2_v': nrm((L, HEAD_DIM, HEAD_DIM), HEAD_DIM ** -0.5),
        'g_out_lru': gain((L, D_LRU)),
        'g_out_nsa': gain((L, D_NSA)),
        'w_out': nrm((L, D_MIX, D_MODEL), D_MIX ** -0.5),
        'g_mem_q': gain((L, D_MODEL)),
        'g_mem_kv': gain((L, D_MODEL)),
        'w_mq': nrm((L, D_MODEL, D_MODEL), D_MODEL ** -0.5),
        'w_mk': nrm((L, D_MODEL, D_MODEL), D_MODEL ** -0.5),
        'w_mv': nrm((L, D_MODEL, D_MODEL), D_MODEL ** -0.5),
        'w_mo': nrm((L, D_MODEL, D_MODEL), D_MODEL ** -0.5),
        'g_ffn': gain((L, D_MODEL)),
        'w_pq': nrm((L, D_MODEL, PEER_HEADS * PEER_QDIM), D_MODEL ** -0.5),
        'sub_keys': nrm((L, PEER_HEADS, 2, PEER_NKEYS, PEER_QDIM // 2), (PEER_QDIM // 2) ** -0.5),
        'peer_u': nrm((L, PEER_N, D_MODEL), D_MODEL ** -0.5),
        'peer_v': nrm((L, PEER_N, D_MODEL), (PEER_HEADS * PEER_TOPK) ** -0.5),
        'g_final': gain((D_MODEL,)),
    }


def reference(x, mem, g_mix, w_in, b_gate, conv_w, conv_b, w_rg_a, b_rg_a, w_rg_i, b_rg_i, lam,
              cmp_pos_k, cmp_w1_k, cmp_w2_k, cmp_pos_v, cmp_w1_v, cmp_w2_v, g_out_lru, g_out_nsa,
              w_out, g_mem_q, g_mem_kv, w_mq, w_mk, w_mv, w_mo, g_ffn, w_pq, sub_keys, peer_u,
              peer_v, g_final):
    B, S, _ = x.shape
    G = N_HEADS // N_KV
    offs = [int(o) for o in np.cumsum(IN_SIZES)[:-1]]

    def kv(t):
        return t.reshape(B, S, N_KV, HEAD_DIM)

    for l in range(DEPTH):
        h = rmsnorm(x, g_mix[l])
        z = h @ w_in[l]
        x_lru, gate_lru, q, k_c, v_c, k_s, v_s, k_w, v_w, g_raw = jnp.split(z, offs, axis=-1)
        y_lru = rg_lru(x_lru, gate_lru, conv_w[l], conv_b[l], w_rg_a[l], b_rg_a[l],
                       w_rg_i[l], b_rg_i[l], lam[l])
        kc = compress(kv(k_c), cmp_pos_k[l], cmp_w1_k[l], cmp_w2_k[l])
        vc = compress(kv(v_c), cmp_pos_v[l], cmp_w1_v[l], cmp_w2_v[l])
        gates = jax.nn.sigmoid(g_raw + b_gate[l]).reshape(B, S, N_KV, G, 3)
        y_nsa = nsa(q.reshape(B, S, N_HEADS, HEAD_DIM), kc, vc, kv(k_s), kv(v_s), kv(k_w), kv(v_w), gates)
        y = jnp.concatenate([rmsnorm(y_lru, g_out_lru[l]), rmsnorm(y_nsa, g_out_nsa[l])], axis=-1)
        x = x + y @ w_out[l]
        x = x + mem_attn(rmsnorm(x, g_mem_q[l]), rmsnorm(mem, g_mem_kv[l]),
                         w_mq[l], w_mk[l], w_mv[l], w_mo[l])
        x = x + peer(rmsnorm(x, g_ffn[l]), w_pq[l], sub_keys[l], peer_u[l], peer_v[l])
    return rmsnorm(x, g_final)
```

```python
import functools

import numpy as np
import jax
import jax.numpy as jnp
from jax import lax
from jax.experimental import pallas as pl
from jax.experimental.pallas import tpu as pltpu

F32 = jnp.float32
BF16 = jnp.bfloat16

D_MODEL = 1024
N_MEM = 256
D_LRU = 512
LRU_BLOCKS = 8
CONV_W = 4
LRU_C = 8.0
N_HEADS = 8
N_KV = 2
GROUP = N_HEADS // N_KV
HEAD_DIM = 64
D_NSA = N_HEADS * HEAD_DIM
CMP_BLOCK = 32
CMP_STRIDE = 16
SEL_BLOCK = 64
N_SELECT = 16
WINDOW = 512
MEM_HEADS = 4
MEM_HEAD_DIM = D_MODEL // MEM_HEADS
PEER_HEADS = 8
PEER_NKEYS = 128
PEER_TOPK = 16
PEER_QDIM = 256
EPS = 1e-6
KV_W = N_KV * HEAD_DIM
N_IN = 2 * D_LRU + D_NSA + 6 * KV_W + 3 * N_HEADS
N_IN_PAD = 2 * D_LRU + D_NSA + 6 * KV_W + 128

LANES = 128
SUBLANES = 8
NEG = -1e30
VMEM_LIMIT = 56 * 1024 * 1024

Q_TILE = 128
K_TILE = 512


def _cparams(sem):
    return pltpu.CompilerParams(dimension_semantics=sem, vmem_limit_bytes=VMEM_LIMIT)


def _rms(x, g):
    return x * lax.rsqrt(jnp.mean(x * x, axis=-1, keepdims=True) + EPS) * g


def _gelu(x):
    c = 0.7978845608028654
    return 0.5 * x * (1.0 + jnp.tanh(c * (x + 0.044715 * (x * x * x))))


def _dot(a, b):
    return jnp.dot(a, b, preferred_element_type=F32)


def _dot_nt(a, b):
    return lax.dot_general(a, b, (((1,), (1,)), ((), ())), preferred_element_type=F32)


def _split3(a):
    hi = a.astype(BF16)
    r1 = a - hi.astype(F32)
    mid = r1.astype(BF16)
    lo = (r1 - mid.astype(F32)).astype(BF16)
    return hi, mid, lo


def _batcher_pairs(n):
    pairs = []
    p = 1
    while p < n:
        k = p
        while k >= 1:
            for j in range(k % p, n - k, 2 * k):
                for i in range(min(k, n - j - k)):
                    if (i + j) // (2 * p) == (i + j + k) // (2 * p):
                        pairs.append((i + j, i + j + k))
            k //= 2
        p *= 2
    return pairs


_SORT16 = _batcher_pairs(16)


def _sort_desc(xs):
    xs = list(xs)
    for i, j in _batcher_pairs(len(xs)):
        a, b = xs[i], xs[j]
        xs[i], xs[j] = jnp.maximum(a, b), jnp.minimum(a, b)
    return xs


def _bitonic_merge_desc(xs):
    xs = list(xs)
    n = len(xs)
    stride = n // 2
    while stride >= 1:
        for i in range(n):
            if (i & stride) == 0:
                a, b = xs[i], xs[i + stride]
                xs[i], xs[i + stride] = jnp.maximum(a, b), jnp.minimum(a, b)
        stride //= 2
    return xs


def _top_half(a, b):
    n = len(a)
    return [jnp.maximum(a[i], b[n - 1 - i]) for i in range(n)]


def _topk_rows(x, *, want_sorted):
    k = PEER_TOPK
    rows = [x[SUBLANES * v:SUBLANES * (v + 1), :] for v in range(k)]
    rows = _sort_desc(rows)
    for step, shift in enumerate((4, 2, 1)):
        part = [pltpu.roll(r, shift, 0) for r in rows]
        rows = _top_half(rows, part)
        if step < 2 or want_sorted:
            rows = _bitonic_merge_desc(rows)
    if want_sorted:
        return rows
    return functools.reduce(jnp.minimum, rows)


def _inproj_kernel(x_ref, g_ref, w_ref, bg_ref,
                   xl_ref, gl_ref, q_ref, kvc_ref, ks_ref, vs_ref, kw_ref, vw_ref, gate_ref):
    h = _rms(x_ref[...], g_ref[...]).astype(BF16)
    z = _dot(h, w_ref[...])
    o = 0
    xl_ref[...] = z[:, o:o + D_LRU]; o += D_LRU
    gl_ref[...] = z[:, o:o + D_LRU]; o += D_LRU
    q_ref[...] = (z[:, o:o + D_NSA] * (HEAD_DIM ** -0.5)).astype(BF16); o += D_NSA
    kvc_ref[...] = z[:, o:o + 2 * KV_W]; o += 2 * KV_W
    ks_ref[...] = z[:, o:o + KV_W].astype(BF16); o += KV_W
    vs_ref[...] = z[:, o:o + KV_W].astype(BF16); o += KV_W
    kw_ref[...] = z[:, o:o + KV_W].astype(BF16); o += KV_W
    vw_ref[...] = z[:, o:o + KV_W].astype(BF16); o += KV_W
    gate_ref[...] = jax.nn.sigmoid(z[:, o:o + LANES] + bg_ref[...])


def _in_proj(x2d, g_mix, w_in, b_gate):
    m = x2d.shape[0]
    tm = 512
    w = jnp.pad(w_in, ((0, 0), (0, N_IN_PAD - N_IN))).astype(BF16)
    bg = jnp.pad(b_gate, (0, LANES - 3 * N_HEADS)).reshape(1, LANES)
    row = lambda n: pl.BlockSpec((tm, n), lambda i: (i, 0))
    full = lambda a: pl.BlockSpec(a.shape, lambda i: (0,) * a.ndim)
    g = g_mix.reshape(1, D_MODEL)
    outs = [(D_LRU, F32), (D_LRU, F32), (D_NSA, BF16), (2 * KV_W, F32), (KV_W, BF16), (KV_W, BF16),
            (KV_W, BF16), (KV_W, BF16), (LANES, F32)]
    return pl.pallas_call(
        _inproj_kernel,
        grid=(m // tm,),
        in_specs=[row(D_MODEL), full(g), full(w), full(bg)],
        out_specs=[row(n) for n, _ in outs],
        out_shape=[jax.ShapeDtypeStruct((m, n), dt) for n, dt in outs],
        compiler_params=_cparams(("parallel",)),
        name="in_proj",
    )(x2d, g, w, bg)


def _rglru_kernel(x_ref, xprev_ref, gate_ref, cw_ref, cb_ref, wa_ref, ba_ref, wi_ref, bi_ref,
                  lam_ref, gout_ref, o_ref, h_ref, *, t):
    i = pl.program_id(1)

    @pl.when(i == 0)
    def _():
        h_ref[...] = jnp.zeros_like(h_ref)

    x = x_ref[...]
    prev = jnp.where(i > 0, xprev_ref[...], 0.0)
    xs = jnp.concatenate([prev, x], axis=0)
    cw = cw_ref[...]
    xc = cb_ref[...] + x * cw[CONV_W - 1:CONV_W, :]
    for k in range(1, CONV_W):
        xk = pltpu.roll(xs, k, 0)[SUBLANES:, :]
        xc = xc + xk * cw[CONV_W - 1 - k:CONV_W - k, :]
    xcb = xc.astype(BF16)
    r = jax.nn.sigmoid(_dot(xcb, wa_ref[...]) + ba_ref[...])
    gi = jax.nn.sigmoid(_dot(xcb, wi_ref[...]) + bi_ref[...])
    lam = lam_ref[...]
    softplus = jnp.maximum(-lam, 0.0) + jnp.log1p(jnp.exp(-jnp.abs(lam)))
    log_a = (-LRU_C) * r * softplus
    a = jnp.exp(log_a)
    b = jnp.sqrt(-jnp.tanh(log_a) * (a * a + 1.0)) * (gi * xc)
    rowi = lax.broadcasted_iota(jnp.int32, (t, D_LRU), 0)
    d = 1
    while d < t:
        ok = rowi >= d
        a_s = pltpu.roll(a, d, 0)
        b_s = pltpu.roll(b, d, 0)
        b = jnp.where(ok, a * b_s, 0.0) + b
        a = jnp.where(ok, a * a_s, a)
        d *= 2
    h = b + a * h_ref[0:1, :]
    h_ref[...] = jnp.broadcast_to(h[t - 1:t, :], h_ref.shape)
    y = h * _gelu(gate_ref[...])
    o_ref[...] = _rms(y, gout_ref[...]).astype(o_ref.dtype)


def _block_diag(w):
    nb, bd, _ = w.shape
    eye = jnp.eye(nb, dtype=w.dtype)
    return (eye[:, None, :, None] * w[:, :, None, :]).reshape(nb * bd, nb * bd)


def _rg_lru(xl, gl, conv_w, conv_b, w_a, b_a, w_i, b_i, lam, g_out, bsz, seq):
    t = 256
    xl3 = xl.reshape(bsz, seq, D_LRU)
    gl3 = gl.reshape(bsz, seq, D_LRU)
    cw = jnp.pad(conv_w, ((0, SUBLANES - CONV_W), (0, 0)))
    vec = lambda a: a.reshape(1, D_LRU)
    wa = _block_diag(w_a).astype(BF16)
    wi = _block_diag(w_i).astype(BF16)
    args = [cw, vec(conv_b), wa, vec(b_a), wi, vec(b_i), vec(lam), vec(g_out)]
    full = lambda a: pl.BlockSpec(a.shape, lambda b, i: (0,) * a.ndim)
    tile = pl.BlockSpec((None, t, D_LRU), lambda b, i: (b, i, 0))
    prev = pl.BlockSpec((None, SUBLANES, D_LRU),
                        lambda b, i: (b, jnp.maximum(i * (t // SUBLANES) - 1, 0), 0))
    return pl.pallas_call(
        functools.partial(_rglru_kernel, t=t),
        grid=(bsz, seq // t),
        in_specs=[tile, prev, tile] + [full(a) for a in args],
        out_specs=tile,
        out_shape=jax.ShapeDtypeStruct((bsz, seq, D_LRU), BF16),
        scratch_shapes=[pltpu.VMEM((SUBLANES, D_LRU), F32)],
        compiler_params=_cparams(("parallel", "arbitrary")),
        name="rg_lru",
    )(xl3, xl3, gl3, *args)


def _compress_kernel(r_ref, pa_ref, pb_ref, w1a_ref, w1b_ref, w2_ref, k_ref, v_ref):
    r = r_ref[...]
    n = r.shape[0]

    def dot3(x, w_ref):
        out = None
        for piece in _split3(x):
            for wi in range(3):
                term = _dot(piece, w_ref[wi])
                out = term if out is None else out + term
        return out

    first = dot3(r + pa_ref[...], w1a_ref)
    second = dot3(r + pb_ref[...], w1b_ref)
    pre = first + pltpu.roll(second, n - 1, 0)
    out = dot3(_gelu(pre), w2_ref)
    k_ref[...] = out[:, :KV_W]
    v_ref[...] = out[:, KV_W:].astype(v_ref.dtype)


def _compress(kvc, pos_k, w1_k, w2_k, pos_v, w1_v, w2_v, bsz, seq):
    nrow = seq // CMP_STRIDE
    width = CMP_STRIDE * 2 * KV_W
    r = kvc.reshape(bsz, nrow, width)
    eye = jnp.eye(N_KV, dtype=F32)

    def big1(w1, half):
        w = w1.reshape(CMP_BLOCK, HEAD_DIM, HEAD_DIM)[half * CMP_STRIDE:(half + 1) * CMP_STRIDE]
        return jnp.einsum("ldo,hg->lhdgo", w, eye).reshape(CMP_STRIDE, KV_W, KV_W)

    def both1(half):
        z = jnp.zeros((CMP_STRIDE, KV_W, KV_W), F32)
        top = jnp.concatenate([big1(w1_k, half), z], axis=2)
        bot = jnp.concatenate([z, big1(w1_v, half)], axis=2)
        return jnp.concatenate([top, bot], axis=1).reshape(width, 2 * KV_W)

    def pos_row(half):
        pk = jnp.tile(pos_k[half * CMP_STRIDE:(half + 1) * CMP_STRIDE, None, :], (1, N_KV, 1))
        pv = jnp.tile(pos_v[half * CMP_STRIDE:(half + 1) * CMP_STRIDE, None, :], (1, N_KV, 1))
        return jnp.concatenate([pk, pv], axis=1).reshape(1, width)

    def w2big():
        zk = jnp.zeros((KV_W, KV_W), F32)
        k2 = jnp.einsum("do,hg->hdgo", w2_k, eye).reshape(KV_W, KV_W)
        v2 = jnp.einsum("do,hg->hdgo", w2_v, eye).reshape(KV_W, KV_W)
        return jnp.concatenate([jnp.concatenate([k2, zk], 1), jnp.concatenate([zk, v2], 1)], 0)

    pieces = lambda w: jnp.stack(_split3(w))
    args = [pos_row(0), pos_row(1), pieces(both1(0)), pieces(both1(1)), pieces(w2big())]
    full = lambda a: pl.BlockSpec(a.shape, lambda b: (0,) * a.ndim)
    return pl.pallas_call(
        _compress_kernel,
        grid=(bsz,),
        in_specs=[pl.BlockSpec((None, nrow, width), lambda b: (b, 0, 0))] + [full(a) for a in args],
        out_specs=[pl.BlockSpec((None, nrow, KV_W), lambda b: (b, 0, 0))] * 2,
        out_shape=[jax.ShapeDtypeStruct((bsz, nrow, KV_W), F32),
                   jax.ShapeDtypeStruct((bsz, nrow, KV_W), BF16)],
        compiler_params=_cparams(("parallel",)),
        name="compress",
    )(r, *args)


def _heads_to_lanes(o):
    return jnp.concatenate([o[g * Q_TILE:(g + 1) * Q_TILE, :] for g in range(GROUP)], axis=1)


def _select_kernel(q_ref, kc_ref, vc_ref, ovl_ref, tri_ref, o_ref, bias_ref, *, nsb):
    s0 = pl.program_id(2) * Q_TILE
    rows = GROUP * Q_TILE
    ncp = kc_ref.shape[0]
    q = q_ref[...].reshape(rows, LANES)
    kc_hi, kc_mid, kc_lo = _split3(kc_ref[...])
    s = _dot_nt(q, kc_hi) + _dot_nt(q, kc_mid) + _dot_nt(q, kc_lo)
    t = s0 + (lax.broadcasted_iota(jnp.int32, (rows, ncp), 0) & (Q_TILE - 1))
    cend = lax.broadcasted_iota(jnp.int32, (rows, ncp), 1) * CMP_STRIDE + (CMP_BLOCK - 1)
    mask = cend <= t
    sm = jnp.where(mask, s, NEG)
    mx = jnp.max(sm, axis=1, keepdims=True)
    p = jnp.where(mask, jnp.exp(sm - mx), 0.0)
    l = jnp.sum(p, axis=1, keepdims=True)
    pc = p / jnp.maximum(l, 1e-30)
    o = _dot(pc.astype(BF16), vc_ref[...])
    o_ref[...] = _heads_to_lanes(o)
    psum = pc[0:Q_TILE]
    for g in range(1, GROUP):
        psum = psum + pc[g * Q_TILE:(g + 1) * Q_TILE]
    ovl = ovl_ref[...]
    imp = None
    for piece in _split3(psum):
        term = _dot(piece, ovl)
        imp = term if imp is None else imp + term
    tq = s0 + lax.broadcasted_iota(jnp.int32, (Q_TILE, nsb), 0)
    jb = lax.broadcasted_iota(jnp.int32, (Q_TILE, nsb), 1)
    cur = tq // SEL_BLOCK
    forced = (jb == 0) | (jb == cur) | (jb == cur - 1)
    val = jnp.where(jb <= cur, jnp.where(forced, jnp.inf, imp), -jnp.inf)
    vt = val.T
    tau = _topk_rows(vt, want_sorted=False)[0:1, :]
    gt = vt > tau
    tie = vt == tau
    need = float(N_SELECT) - jnp.sum(gt.astype(F32), axis=0, keepdims=True)
    pre = _dot(tri_ref[...], tie.astype(BF16))
    sel = (gt | (tie & (pre <= need))) & (vt > -1.0)
    bias_ref[...] = jnp.where(sel, 0.0, NEG).T.astype(bias_ref.dtype)


def _nsa_select(q4, kc4, vc4, bsz, seq):
    assert seq // SEL_BLOCK <= LANES, "block-selection kernel holds at most 128 selection blocks"
    nsb = LANES
    ncp = kc4.shape[2]
    ci = np.arange(ncp)[:, None] * CMP_STRIDE
    bj = np.arange(nsb)[None, :] * SEL_BLOCK
    ovl = ((ci < bj + SEL_BLOCK) & (ci + CMP_BLOCK > bj) & (np.arange(ncp)[:, None] < ncp - 1))
    ovl = jnp.asarray(ovl, BF16)
    tri = jnp.asarray(np.tril(np.ones((nsb, nsb))), BF16)
    full = lambda a: pl.BlockSpec(a.shape, lambda b, k, i: (0,) * a.ndim)
    return pl.pallas_call(
        functools.partial(_select_kernel, nsb=nsb),
        grid=(bsz, N_KV, seq // Q_TILE),
        in_specs=[pl.BlockSpec((None, None, GROUP, Q_TILE, LANES), lambda b, k, i: (b, k, 0, i, 0)),
                  pl.BlockSpec((None, None, ncp, LANES), lambda b, k, i: (b, k, 0, 0)),
                  pl.BlockSpec((None, None, ncp, HEAD_DIM), lambda b, k, i: (b, k, 0, 0)),
                  full(ovl), full(tri)],
        out_specs=[pl.BlockSpec((None, Q_TILE, GROUP * HEAD_DIM), lambda b, k, i: (b, i, k)),
                   pl.BlockSpec((None, None, Q_TILE, nsb), lambda b, k, i: (b, k, i, 0))],
        out_shape=[jax.ShapeDtypeStruct((bsz, seq, D_NSA), F32),
                   jax.ShapeDtypeStruct((bsz, N_KV, seq, nsb), BF16)],
        compiler_params=_cparams(("parallel", "parallel", "parallel")),
        name="nsa_select",
    )(q4, kc4, vc4, ovl, tri)


def _attend_kernel(q_ref, bias_ref, ka_ref, va_ref, kw_ref, vw_ref, os_ref, ow_ref):
    s0 = pl.program_id(2) * Q_TILE
    rows = GROUP * Q_TILE
    q = q_ref[...].reshape(rows, LANES)
    bias = bias_ref[...]
    qa = jnp.concatenate([q, jnp.concatenate([bias] * GROUP, axis=0)], axis=1)
    t_row = s0 + (lax.broadcasted_iota(jnp.int32, (rows, 1), 0) & (Q_TILE - 1))

    def step(kt, carry, masked):
        m, acc = carry
        k0 = pl.multiple_of(kt * K_TILE, K_TILE)
        s = _dot_nt(qa, ka_ref[pl.ds(k0, K_TILE), :])
        if masked:
            kpos = k0 + lax.broadcasted_iota(jnp.int32, (rows, K_TILE), 1)
            s = jnp.where(kpos <= t_row, s, NEG)
        mn = jnp.maximum(m, jnp.max(s, axis=1, keepdims=True))
        p = jnp.exp(s - mn)
        acc = jnp.exp(m - mn) * acc + _dot(p.astype(BF16), va_ref[pl.ds(k0, K_TILE), :])
        return mn, acc

    n_full = s0 // K_TILE
    init = (jnp.full((rows, 1), NEG, F32), jnp.zeros((rows, LANES), F32))
    carry = lax.fori_loop(0, n_full, lambda kt, c: step(kt, c, False), init)
    _, acc = step(n_full, carry, True)
    os_ref[...] = _heads_to_lanes(acc[:, :HEAD_DIM] / acc[:, HEAD_DIM:HEAD_DIM + 1])

    span = WINDOW + Q_TILE
    w0 = pl.multiple_of(jnp.maximum(s0 - WINDOW, 0), Q_TILE)
    s = _dot_nt(q, kw_ref[pl.ds(w0, span), :])
    kpos = w0 + lax.broadcasted_iota(jnp.int32, (rows, span), 1)
    mask = (kpos <= t_row) & (kpos > t_row - WINDOW)
    s = jnp.where(mask, s, NEG)
    p = jnp.where(mask, jnp.exp(s - jnp.max(s, axis=1, keepdims=True)), 0.0)
    acc = _dot(p.astype(BF16), vw_ref[pl.ds(w0, span), :])
    ow_ref[...] = _heads_to_lanes(acc[:, :HEAD_DIM] / acc[:, HEAD_DIM:HEAD_DIM + 1])


def _nsa_attend(q4, bias, ka, va, kw, vw, bsz, seq):
    per_head = lambda n: pl.BlockSpec((None, None, seq, n), lambda b, k, i: (b, k, 0, 0))
    out = pl.BlockSpec((None, Q_TILE, GROUP * HEAD_DIM), lambda b, k, i: (b, i, k))
    return pl.pallas_call(
        _attend_kernel,
        grid=(bsz, N_KV, seq // Q_TILE),
        in_specs=[pl.BlockSpec((None, None, GROUP, Q_TILE, LANES), lambda b, k, i: (b, k, 0, i, 0)),
                  pl.BlockSpec((None, None, Q_TILE, LANES), lambda b, k, i: (b, k, i, 0)),
                  per_head(2 * LANES), per_head(LANES), per_head(LANES), per_head(LANES)],
        out_specs=[out, out],
        out_shape=[jax.ShapeDtypeStruct((bsz, seq, D_NSA), F32)] * 2,
        compiler_params=_cparams(("parallel", "parallel", "arbitrary")),
        name="nsa_attend",
    )(q4, bias, ka, va, kw, vw)


def _memkv_kernel(m_ref, g_ref, wk_ref, wv_ref, k_ref, v_ref):
    mn = _rms(m_ref[...], g_ref[...]).astype(BF16)
    k_ref[...] = _dot(mn, wk_ref[...]).astype(k_ref.dtype)
    v_ref[...] = _dot(mn, wv_ref[...]).astype(v_ref.dtype)


def _mem_kv(mem, g, wk, wv):
    bsz, nm, _ = mem.shape
    g2 = g.reshape(1, D_MODEL)
    full = lambda a: pl.BlockSpec(a.shape, lambda b: (0,) * a.ndim)
    blk = pl.BlockSpec((None, nm, D_MODEL), lambda b: (b, 0, 0))
    return pl.pallas_call(
        _memkv_kernel,
        grid=(bsz,),
        in_specs=[blk, full(g2), full(wk), full(wv)],
        out_specs=[blk, blk],
        out_shape=[jax.ShapeDtypeStruct(mem.shape, BF16)] * 2,
        compiler_params=_cparams(("parallel",)),
        name="mem_kv",
    )(mem, g2, wk, wv)


def _mix_kernel(x_ref, ylru_ref, oc_ref, os_ref, ow_ref, gate_ref, e_ref, gnsa_ref, wout_ref,
                gq_ref, wq_ref, mk_ref, mv_ref, wo_ref, o_ref):
    g = gate_ref[...]
    g_hi = g.astype(BF16)
    g_lo = (g - g_hi.astype(F32)).astype(BF16)
    e = e_ref[...]
    gx = _dot(g_hi, e) + _dot(g_lo, e)
    y = (gx[:, 0:D_NSA] * oc_ref[...] + gx[:, D_NSA:2 * D_NSA] * os_ref[...]
         + gx[:, 2 * D_NSA:3 * D_NSA] * ow_ref[...])
    yn = _rms(y, gnsa_ref[...]).astype(BF16)
    ycat = jnp.concatenate([ylru_ref[...], yn], axis=1)
    x1 = x_ref[...] + _dot(ycat, wout_ref[...])
    xq = _rms(x1, gq_ref[...]).astype(BF16)
    qm = _dot(xq, wq_ref[...]).astype(BF16)
    outs = []
    for h in range(MEM_HEADS):
        sl = slice(h * MEM_HEAD_DIM, (h + 1) * MEM_HEAD_DIM)
        s = _dot_nt(qm[:, sl], mk_ref[:, sl])
        p = jnp.exp(s - jnp.max(s, axis=1, keepdims=True))
        p = p / jnp.sum(p, axis=1, keepdims=True)
        outs.append(_dot(p.astype(BF16), mv_ref[:, sl]))
    o = jnp.concatenate(outs, axis=1).astype(BF16)
    o_ref[...] = x1 + _dot(o, wo_ref[...])


def _mix_mem(x3, ylru, oc, osl, ow, gates, g_nsa, w_out, g_q, w_mq, mk, mv, w_mo):
    bsz, seq, _ = x3.shape
    tm = 256
    e = np.zeros((LANES, 3 * D_NSA), np.float32)
    for h in range(N_HEADS):
        for j in range(3):
            e[h * 3 + j, j * D_NSA + h * HEAD_DIM:j * D_NSA + (h + 1) * HEAD_DIM] = 1.0
    e = jnp.asarray(e, BF16)
    vec = lambda a: a.reshape(1, -1)
    wq = (w_mq * (MEM_HEAD_DIM ** -0.5)).astype(BF16)
    row = lambda n: pl.BlockSpec((None, tm, n), lambda b, i: (b, i, 0))
    full = lambda a: pl.BlockSpec(a.shape, lambda b, i: (0,) * a.ndim)
    memb = pl.BlockSpec((None, mk.shape[1], D_MODEL), lambda b, i: (b, 0, 0))
    consts = [e, vec(g_nsa), w_out.astype(BF16), vec(g_q), wq]
    return pl.pallas_call(
        _mix_kernel,
        grid=(bsz, seq // tm),
        in_specs=[row(D_MODEL), row(D_LRU), row(D_NSA), row(D_NSA), row(D_NSA), row(LANES)]
        + [full(a) for a in consts] + [memb, memb, full(w_mo)],
        out_specs=row(D_MODEL),
        out_shape=jax.ShapeDtypeStruct(x3.shape, F32),
        compiler_params=_cparams(("parallel", "parallel")),
        name="mix_mem",
    )(x3, ylru, oc, osl, ow, gates.reshape(bsz, seq, LANES), *consts, mk, mv, w_mo)


def _peer_pairs():
    k = PEER_TOPK
    return [(a, b) for a in range(k) for b in range(k) if (a + 1) * (b + 1) <= k]


def _route_kernel(x_ref, g_ref, wpq_ref, sk_ref, xn_ref, r2_ref, e2_ref, ns_ref, sc_ref,
                  q_scr, s_scr, l_scr, t_scr, *, tt):
    k = PEER_TOPK
    nhc = 2 * PEER_HEADS
    xn = _rms(x_ref[...], g_ref[...]).astype(BF16)
    xn_ref[...] = xn
    q = _dot(xn, wpq_ref[...]).astype(BF16)
    for hc in range(nhc):
        q_scr[hc] = q[:, hc * PEER_NKEYS:(hc + 1) * PEER_NKEYS]

    def score_and_sort(hc, carry):
        st = _dot_nt(sk_ref[hc], q_scr[hc])
        s_scr[hc] = st
        rows = _topk_rows(st, want_sorted=True)
        h, c = hc // 2, hc % 2
        for a in range(k):
            l_scr[c, a, pl.ds(h, 1), :] = rows[a][0:1, :]
        return carry

    lax.fori_loop(0, nhc, score_and_sort, 0)
    v1 = [l_scr[0, a] for a in range(k)]
    v2 = [l_scr[1, b] for b in range(k)]
    pairs = _peer_pairs()
    cand = [v1[a] + v2[b] for a, b in pairs]
    pad = [jnp.full_like(cand[0], -jnp.inf)] * (4 * k - len(cand))
    groups = [_sort_desc((cand + pad)[i * k:(i + 1) * k]) for i in range(4)]
    left = _bitonic_merge_desc(_top_half(groups[0], groups[1]))
    right = _bitonic_merge_desc(_top_half(groups[2], groups[3]))
    tau = functools.reduce(jnp.minimum, _top_half(left, right))
    top = v1[0] + v2[0]
    z = jnp.zeros_like(top)
    for c_ in cand:
        z = z + jnp.where(c_ >= tau, jnp.exp(c_ - top), 0.0)
    t_scr[0] = tau
    t_scr[1] = 1.0 / z

    def per_head(h, carry):
        s1, s2 = s_scr[2 * h], s_scr[2 * h + 1]
        tau_h = t_scr[0, pl.ds(h, 1), :]
        zinv_h = t_scr[1, pl.ds(h, 1), :]
        rank = jnp.zeros((PEER_NKEYS, tt), F32)
        cnt = jnp.zeros((PEER_NKEYS, tt), F32)
        for b in range(k):
            lb = l_scr[1, b, pl.ds(h, 1), :]
            rank = rank + (lb > s2).astype(F32)
            cnt = cnt + ((s1 + lb) >= tau_h).astype(F32)
        r2_ref[h] = rank.astype(r2_ref.dtype)
        ns_ref[h] = cnt
        e2_ref[h] = jnp.exp(s2 - l_scr[1, 0, pl.ds(h, 1), :]).astype(e2_ref.dtype)
        sc_ref[h] = jnp.exp(s1 - l_scr[0, 0, pl.ds(h, 1), :]) * zinv_h
        return carry

    lax.fori_loop(0, PEER_HEADS, per_head, 0)


def _peer_route(x2d, g_ffn, w_pq, sub_keys):
    m = x2d.shape[0]
    tt = 256
    g = g_ffn.reshape(1, D_MODEL)
    wpq = w_pq.astype(BF16)
    sk = sub_keys.astype(BF16).reshape(2 * PEER_HEADS, PEER_NKEYS, PEER_QDIM // 2)
    full = lambda a: pl.BlockSpec(a.shape, lambda i: (0,) * a.ndim)
    tcol = pl.BlockSpec((PEER_HEADS, PEER_NKEYS, tt), lambda i: (0, 0, i))
    shp = lambda dt: jax.ShapeDtypeStruct((PEER_HEADS, PEER_NKEYS, m), dt)
    return pl.pallas_call(
        functools.partial(_route_kernel, tt=tt),
        grid=(m // tt,),
        in_specs=[pl.BlockSpec((tt, D_MODEL), lambda i: (i, 0)), full(g), full(wpq), full(sk)],
        out_specs=[pl.BlockSpec((tt, D_MODEL), lambda i: (i, 0)), tcol, tcol, tcol, tcol],
        out_shape=[jax.ShapeDtypeStruct((m, D_MODEL), BF16), shp(BF16), shp(BF16), shp(F32), shp(F32)],
        scratch_shapes=[pltpu.VMEM((2 * PEER_HEADS, tt, PEER_NKEYS), BF16),
                        pltpu.VMEM((2 * PEER_HEADS, PEER_NKEYS, tt), F32),
                        pltpu.VMEM((2, PEER_TOPK, PEER_HEADS, tt), F32),
                        pltpu.VMEM((2, PEER_HEADS, tt), F32)],
        compiler_params=_cparams(("parallel",)),
        name="peer_route",
    )(x2d, g, wpq, sk)


def _dense_kernel(x_ref, xn_ref, r2_ref, e2_ref, ns_ref, sc_ref, u_ref, vt_ref, gf_ref,
                  o_ref, acc_ref, a_ref, *, i1b, tt):
    ei = pl.program_id(1)

    @pl.when(ei == 0)
    def _():
        acc_ref[...] = jnp.zeros_like(acc_ref)

    ht = _dot_nt(u_ref[...], xn_ref[...])
    gh = _gelu(ht)
    for kk in range(i1b):
        i1 = ei * i1b + kk
        w = jnp.zeros((PEER_NKEYS, tt), F32)
        for h in range(PEER_HEADS):
            n_row = ns_ref[h, pl.ds(i1, 1), :]
            s_row = sc_ref[h, pl.ds(i1, 1), :]
            hit = r2_ref[h].astype(F32) < n_row
            w = w + jnp.where(hit, e2_ref[h].astype(F32), 0.0) * s_row
        rs = slice(kk * PEER_NKEYS, (kk + 1) * PEER_NKEYS)
        a_ref[rs, :] = (w * gh[rs, :]).astype(a_ref.dtype)
    acc_ref[...] += _dot(vt_ref[...], a_ref[...])

    @pl.when(ei == pl.num_programs(1) - 1)
    def _():
        o_ref[...] = _rms(x_ref[...] + acc_ref[...].T, gf_ref[...])


def _peer_dense(x2d, xn, r2, e2, ns, sc, peer_u, peer_v, g_final):
    m = x2d.shape[0]
    tt = min(512, m)
    i1b = 4
    te = i1b * PEER_NKEYS
    u = peer_u.astype(BF16)
    vt = peer_v.T.astype(BF16)
    gf = g_final.reshape(1, D_MODEL)
    tok = pl.BlockSpec((tt, D_MODEL), lambda t, e: (t, 0))
    tcol = pl.BlockSpec((PEER_HEADS, PEER_NKEYS, tt), lambda t, e: (0, 0, t))
    return pl.pallas_call(
        functools.partial(_dense_kernel, i1b=i1b, tt=tt),
        grid=(m // tt, PEER_NKEYS // i1b),
        in_specs=[tok, tok, tcol, tcol, tcol, tcol,
                  pl.BlockSpec((te, D_MODEL), lambda t, e: (e, 0)),
                  pl.BlockSpec((D_MODEL, te), lambda t, e: (0, e)),
                  pl.BlockSpec((1, D_MODEL), lambda t, e: (0, 0))],
        out_specs=tok,
        out_shape=jax.ShapeDtypeStruct((m, D_MODEL), F32),
        scratch_shapes=[pltpu.VMEM((D_MODEL, tt), F32), pltpu.VMEM((te, tt), BF16)],
        compiler_params=_cparams(("parallel", "arbitrary")),
        name="peer_dense",
    )(x2d, xn, r2, e2, ns, sc, u, vt, gf)


def _per_kv_head(a, bsz, seq):
    return a.reshape(bsz, seq, N_KV, HEAD_DIM).transpose(0, 2, 1, 3)


def _layer(x, mem, p):
    bsz, seq, _ = x.shape
    m = bsz * seq
    xl, gl, q, kvc, ks, vs, kw, vw, gates = _in_proj(x.reshape(m, D_MODEL), p["g_mix"], p["w_in"],
                                                     p["b_gate"])
    ylru = _rg_lru(xl, gl, p["conv_w"], p["conv_b"], p["w_rg_a"], p["b_rg_a"], p["w_rg_i"],
                   p["b_rg_i"], p["lam"], p["g_out_lru"], bsz, seq)
    kc, vc = _compress(kvc, p["cmp_pos_k"], p["cmp_w1_k"], p["cmp_w2_k"], p["cmp_pos_v"],
                       p["cmp_w1_v"], p["cmp_w2_v"], bsz, seq)
    ncp = seq // CMP_STRIDE
    pad64 = lambda a: jnp.pad(a, ((0, 0),) * (a.ndim - 1) + ((0, LANES - HEAD_DIM),))
    kc4 = pad64(kc.reshape(bsz, ncp, N_KV, HEAD_DIM).transpose(0, 2, 1, 3))
    vc4 = vc.reshape(bsz, ncp, N_KV, HEAD_DIM).transpose(0, 2, 1, 3)
    q4 = pad64(q.reshape(bsz, seq, N_KV, GROUP, HEAD_DIM).transpose(0, 2, 3, 1, 4))
    o_cmp, bias = _nsa_select(q4, kc4, vc4, bsz, seq)

    nsb = LANES
    onehot = jnp.asarray(np.arange(seq)[:, None] // SEL_BLOCK == np.arange(nsb)[None, :], BF16)
    ones_col = jnp.asarray(np.arange(LANES - HEAD_DIM)[None, :] == 0, BF16)

    def with_ones(v):
        v4 = _per_kv_head(v, bsz, seq)
        return jnp.concatenate([v4, jnp.broadcast_to(ones_col, v4.shape[:3] + (LANES - HEAD_DIM,))], -1)

    ka = jnp.concatenate([pad64(_per_kv_head(ks, bsz, seq)),
                          jnp.broadcast_to(onehot, (bsz, N_KV, seq, nsb))], axis=-1)
    o_slc, o_win = _nsa_attend(q4, bias, ka, with_ones(vs), pad64(_per_kv_head(kw, bsz, seq)),
                               with_ones(vw), bsz, seq)

    mk, mv = _mem_kv(mem, p["g_mem_kv"], p["w_mk"].astype(BF16), p["w_mv"].astype(BF16))
    x2 = _mix_mem(x, ylru, o_cmp, o_slc, o_win, gates, p["g_out_nsa"], p["w_out"], p["g_mem_q"],
                  p["w_mq"], mk, mv, p["w_mo"].astype(BF16))
    return x2.reshape(m, D_MODEL)


def kernel(x, mem, g_mix, w_in, b_gate, conv_w, conv_b, w_rg_a, b_rg_a, w_rg_i, b_rg_i, lam,
           cmp_pos_k, cmp_w1_k, cmp_w2_k, cmp_pos_v, cmp_w1_v, cmp_w2_v, g_out_lru, g_out_nsa,
           w_out, g_mem_q, g_mem_kv, w_mq, w_mk, w_mv, w_mo, g_ffn, w_pq, sub_keys, peer_u,
           peer_v, g_final):
    depth = w_in.shape[0]
    assert depth == 1, "the final norm is fused into the single layer's PEER kernel"
    names = ["g_mix", "w_in", "b_gate", "conv_w", "conv_b", "w_rg_a", "b_rg_a", "w_rg_i", "b_rg_i",
             "lam", "cmp_pos_k", "cmp_w1_k", "cmp_w2_k", "cmp_pos_v", "cmp_w1_v", "cmp_w2_v",
             "g_out_lru", "g_out_nsa", "w_out", "g_mem_q", "g_mem_kv", "w_mq", "w_mk", "w_mv", "w_mo",
             "g_ffn", "w_pq", "sub_keys", "peer_u", "peer_v"]
    vals = [g_mix, w_in, b_gate, conv_w, conv_b, w_rg_a, b_rg_a, w_rg_i, b_rg_i, lam, cmp_pos_k,
            cmp_w1_k, cmp_w2_k, cmp_pos_v, cmp_w1_v, cmp_w2_v, g_out_lru, g_out_nsa, w_out, g_mem_q,
            g_mem_kv, w_mq, w_mk, w_mv, w_mo, g_ffn, w_pq, sub_keys, peer_u, peer_v]
    p = {n: v[0] for n, v in zip(names, vals)}
    bsz, seq, _ = x.shape
    x2 = _layer(x, mem, p)
    xn, r2, e2, ns, sc = _peer_route(x2, p["g_ffn"], p["w_pq"], p["sub_keys"])
    out = _peer_dense(x2, xn, r2, e2, ns, sc, p["peer_u"], p["peer_v"], g_final)
    return out.reshape(bsz, seq, D_MODEL)
```

```python
import functools

import numpy as np
import jax
import jax.numpy as jnp
from jax import lax
from jax.experimental import pallas as pl
from jax.experimental.pallas import tpu as pltpu

F32 = jnp.float32
BF16 = jnp.bfloat16

D_MODEL = 1024
N_MEM = 256
D_LRU = 512
LRU_BLOCKS = 8
CONV_W = 4
LRU_C = 8.0
N_HEADS = 8
N_KV = 2
GROUP = N_HEADS // N_KV
HEAD_DIM = 64
D_NSA = N_HEADS * HEAD_DIM
CMP_BLOCK = 32
CMP_STRIDE = 16
SEL_BLOCK = 64
N_SELECT = 16
WINDOW = 512
MEM_HEADS = 4
MEM_HEAD_DIM = D_MODEL // MEM_HEADS
PEER_HEADS = 8
PEER_NKEYS = 128
PEER_TOPK = 16
PEER_QDIM = 256
EPS = 1e-6
KV_W = N_KV * HEAD_DIM
N_IN = 2 * D_LRU + D_NSA + 6 * KV_W + 3 * N_HEADS
N_IN_PAD = 2 * D_LRU + D_NSA + 6 * KV_W + 128

LANES = 128
SUBLANES = 8
NEG = -1e30
VMEM_LIMIT = 56 * 1024 * 1024

Q_TILE = 128
K_TILE = 512


def _cparams(sem):
    return pltpu.CompilerParams(dimension_semantics=sem, vmem_limit_bytes=VMEM_LIMIT)


def _rms(x, g):
    return x * lax.rsqrt(jnp.mean(x * x, axis=-1, keepdims=True) + EPS) * g


def _gelu(x):
    c = 0.7978845608028654
    return 0.5 * x * (1.0 + jnp.tanh(c * (x + 0.044715 * (x * x * x))))


def _dot(a, b):
    return jnp.dot(a, b, preferred_element_type=F32)


def _dot_nt(a, b):
    return lax.dot_general(a, b, (((1,), (1,)), ((), ())), preferred_element_type=F32)


def _split3(a):
    hi = a.astype(BF16)
    r1 = a - hi.astype(F32)
    mid = r1.astype(BF16)
    lo = (r1 - mid.astype(F32)).astype(BF16)
    return hi, mid, lo


def _batcher_pairs(n):
    pairs = []
    p = 1
    while p < n:
        k = p
        while k >= 1:
            for j in range(k % p, n - k, 2 * k):
                for i in range(min(k, n - j - k)):
                    if (i + j) // (2 * p) == (i + j + k) // (2 * p):
                        pairs.append((i + j, i + j + k))
            k //= 2
        p *= 2
    return pairs


_SORT16 = _batcher_pairs(16)


def _sort_desc(xs):
    xs = list(xs)
    for i, j in _batcher_pairs(len(xs)):
        a, b = xs[i], xs[j]
        xs[i], xs[j] = jnp.maximum(a, b), jnp.minimum(a, b)
    return xs


def _bitonic_merge_desc(xs):
    xs = list(xs)
    n = len(xs)
    stride = n // 2
    while stride >= 1:
        for i in range(n):
            if (i & stride) == 0:
                a, b = xs[i], xs[i + stride]
                xs[i], xs[i + stride] = jnp.maximum(a, b), jnp.minimum(a, b)
        stride //= 2
    return xs


def _top_half(a, b):
    n = len(a)
    return [jnp.maximum(a[i], b[n - 1 - i]) for i in range(n)]


def _topk_rows(x, *, want_sorted):
    k = PEER_TOPK
    rows = [x[SUBLANES * v:SUBLANES * (v + 1), :] for v in range(k)]
    rows = _sort_desc(rows)
    for step, shift in enumerate((4, 2, 1)):
        part = [pltpu.roll(r, shift, 0) for r in rows]
        rows = _top_half(rows, part)
        if step < 2 or want_sorted:
            rows = _bitonic_merge_desc(rows)
    if want_sorted:
        return rows
    return functools.reduce(jnp.minimum, rows)


def _inproj_kernel(x_ref, g_ref, w_ref, bg_ref,
                   xl_ref, gl_ref, q_ref, kvc_ref, ks_ref, vs_ref, kw_ref, vw_ref, gate_ref):
    h = _rms(x_ref[...], g_ref[...]).astype(BF16)
    z = _dot(h, w_ref[...])
    o = 0
    xl_ref[...] = z[:, o:o + D_LRU]; o += D_LRU
    gl_ref[...] = z[:, o:o + D_LRU]; o += D_LRU
    q_ref[...] = (z[:, o:o + D_NSA] * (HEAD_DIM ** -0.5)).astype(BF16); o += D_NSA
    kvc_ref[...] = z[:, o:o + 2 * KV_W]; o += 2 * KV_W
    ks_ref[...] = z[:, o:o + KV_W].astype(BF16); o += KV_W
    vs_ref[...] = z[:, o:o + KV_W].astype(BF16); o += KV_W
    kw_ref[...] = z[:, o:o + KV_W].astype(BF16); o += KV_W
    vw_ref[...] = z[:, o:o + KV_W].astype(BF16); o += KV_W
    gate_ref[...] = jax.nn.sigmoid(z[:, o:o + LANES] + bg_ref[...])


def _in_proj(x2d, g_mix, w_in, b_gate):
    m = x2d.shape[0]
    tm = 512
    w = jnp.pad(w_in, ((0, 0), (0, N_IN_PAD - N_IN))).astype(BF16)
    bg = jnp.pad(b_gate, (0, LANES - 3 * N_HEADS)).reshape(1, LANES)
    row = lambda n: pl.BlockSpec((tm, n), lambda i: (i, 0))
    full = lambda a: pl.BlockSpec(a.shape, lambda i: (0,) * a.ndim)
    g = g_mix.reshape(1, D_MODEL)
    outs = [(D_LRU, F32), (D_LRU, F32), (D_NSA, BF16), (2 * KV_W, F32), (KV_W, BF16), (KV_W, BF16),
            (KV_W, BF16), (KV_W, BF16), (LANES, F32)]
    return pl.pallas_call(
        _inproj_kernel,
        grid=(m // tm,),
        in_specs=[row(D_MODEL), full(g), full(w), full(bg)],
        out_specs=[row(n) for n, _ in outs],
        out_shape=[jax.ShapeDtypeStruct((m, n), dt) for n, dt in outs],
        compiler_params=_cparams(("parallel",)),
        name="in_proj",
    )(x2d, g, w, bg)


def _rglru_kernel(x_ref, xprev_ref, gate_ref, cw_ref, cb_ref, wa_ref, ba_ref, wi_ref, bi_ref,
                  lam_ref, gout_ref, o_ref, h_ref, *, t):
    i = pl.program_id(1)

    @pl.when(i == 0)
    def _():
        h_ref[...] = jnp.zeros_like(h_ref)

    x = x_ref[...]
    prev = jnp.where(i > 0, xprev_ref[...], 0.0)
    xs = jnp.concatenate([prev, x], axis=0)
    cw = cw_ref[...]
    xc = cb_ref[...] + x * cw[CONV_W - 1:CONV_W, :]
    for k in range(1, CONV_W):
        xk = pltpu.roll(xs, k, 0)[SUBLANES:, :]
        xc = xc + xk * cw[CONV_W - 1 - k:CONV_W - k, :]
    xcb = xc.astype(BF16)
    r = jax.nn.sigmoid(_dot(xcb, wa_ref[...]) + ba_ref[...])
    gi = jax.nn.sigmoid(_dot(xcb, wi_ref[...]) + bi_ref[...])
    lam = lam_ref[...]
    softplus = jnp.maximum(-lam, 0.0) + jnp.log1p(jnp.exp(-jnp.abs(lam)))
    log_a = (-LRU_C) * r * softplus
    a = jnp.exp(log_a)
    b = jnp.sqrt(-jnp.tanh(log_a) * (a * a + 1.0)) * (gi * xc)
    rowi = lax.broadcasted_iota(jnp.int32, (t, D_LRU), 0)
    d = 1
    while d < t:
        ok = rowi >= d
        a_s = pltpu.roll(a, d, 0)
        b_s = pltpu.roll(b, d, 0)
        b = jnp.where(ok, a * b_s, 0.0) + b
        a = jnp.where(ok, a * a_s, a)
        d *= 2
    h = b + a * h_ref[0:1, :]
    h_ref[...] = jnp.broadcast_to(h[t - 1:t, :], h_ref.shape)
    y = h * _gelu(gate_ref[...])
    o_ref[...] = _rms(y, gout_ref[...]).astype(o_ref.dtype)


def _block_diag(w):
    nb, bd, _ = w.shape
    eye = jnp.eye(nb, dtype=w.dtype)
    return (eye[:, None, :, None] * w[:, :, None, :]).reshape(nb * bd, nb * bd)


def _rg_lru(xl, gl, conv_w, conv_b, w_a, b_a, w_i, b_i, lam, g_out, bsz, seq):
    t = 256
    xl3 = xl.reshape(bsz, seq, D_LRU)
    gl3 = gl.reshape(bsz, seq, D_LRU)
    cw = jnp.pad(conv_w, ((0, SUBLANES - CONV_W), (0, 0)))
    vec = lambda a: a.reshape(1, D_LRU)
    wa = _block_diag(w_a).astype(BF16)
    wi = _block_diag(w_i).astype(BF16)
    args = [cw, vec(conv_b), wa, vec(b_a), wi, vec(b_i), vec(lam), vec(g_out)]
    full = lambda a: pl.BlockSpec(a.shape, lambda b, i: (0,) * a.ndim)
    tile = pl.BlockSpec((None, t, D_LRU), lambda b, i: (b, i, 0))
    prev = pl.BlockSpec((None, SUBLANES, D_LRU),
                        lambda b, i: (b, jnp.maximum(i * (t // SUBLANES) - 1, 0), 0))
    return pl.pallas_call(
        functools.partial(_rglru_kernel, t=t),
        grid=(bsz, seq // t),
        in_specs=[tile, prev, tile] + [full(a) for a in args],
        out_specs=tile,
        out_shape=jax.ShapeDtypeStruct((bsz, seq, D_LRU), BF16),
        scratch_shapes=[pltpu.VMEM((SUBLANES, D_LRU), F32)],
        compiler_params=_cparams(("parallel", "arbitrary")),
        name="rg_lru",
    )(xl3, xl3, gl3, *args)


def _compress_kernel(r_ref, pa_ref, pb_ref, w1a_ref, w1b_ref, w2_ref, k_ref, v_ref):
    r = r_ref[...]
    n = r.shape[0]

    def dot3(x, w_ref):
        out = None
        for piece in _split3(x):
            for wi in range(3):
                term = _dot(piece, w_ref[wi])
                out = term if out is None else out + term
        return out

    first = dot3(r + pa_ref[...], w1a_ref)
    second = dot3(r + pb_ref[...], w1b_ref)
    pre = first + pltpu.roll(second, n - 1, 0)
    out = dot3(_gelu(pre), w2_ref)
    k_ref[...] = out[:, :KV_W]
    v_ref[...] = out[:, KV_W:].astype(v_ref.dtype)


def _compress(kvc, pos_k, w1_k, w2_k, pos_v, w1_v, w2_v, bsz, seq):
    nrow = seq // CMP_STRIDE
    width = CMP_STRIDE * 2 * KV_W
    r = kvc.reshape(bsz, nrow, width)
    eye = jnp.eye(N_KV, dtype=F32)

    def big1(w1, half):
        w = w1.reshape(CMP_BLOCK, HEAD_DIM, HEAD_DIM)[half * CMP_STRIDE:(half + 1) * CMP_STRIDE]
        return jnp.einsum("ldo,hg->lhdgo", w, eye).reshape(CMP_STRIDE, KV_W, KV_W)

    def both1(half):
        z = jnp.zeros((CMP_STRIDE, KV_W, KV_W), F32)
        top = jnp.concatenate([big1(w1_k, half), z], axis=2)
        bot = jnp.concatenate([z, big1(w1_v, half)], axis=2)
        return jnp.concatenate([top, bot], axis=1).reshape(width, 2 * KV_W)

    def pos_row(half):
        pk = jnp.tile(pos_k[half * CMP_STRIDE:(half + 1) * CMP_STRIDE, None, :], (1, N_KV, 1))
        pv = jnp.tile(pos_v[half * CMP_STRIDE:(half + 1) * CMP_STRIDE, None, :], (1, N_KV, 1))
        return jnp.concatenate([pk, pv], axis=1).reshape(1, width)

    def w2big():
        zk = jnp.zeros((KV_W, KV_W), F32)
        k2 = jnp.einsum("do,hg->hdgo", w2_k, eye).reshape(KV_W, KV_W)
        v2 = jnp.einsum("do,hg->hdgo", w2_v, eye).reshape(KV_W, KV_W)
        return jnp.concatenate([jnp.concatenate([k2, zk], 1), jnp.concatenate([zk, v2], 1)], 0)

    pieces = lambda w: jnp.stack(_split3(w))
    args = [pos_row(0), pos_row(1), pieces(both1(0)), pieces(both1(1)), pieces(w2big())]
    full = lambda a: pl.BlockSpec(a.shape, lambda b: (0,) * a.ndim)
    return pl.pallas_call(
        _compress_kernel,
        grid=(bsz,),
        in_specs=[pl.BlockSpec((None, nrow, width), lambda b: (b, 0, 0))] + [full(a) for a in args],
        out_specs=[pl.BlockSpec((None, nrow, KV_W), lambda b: (b, 0, 0))] * 2,
        out_shape=[jax.ShapeDtypeStruct((bsz, nrow, KV_W), F32),
                   jax.ShapeDtypeStruct((bsz, nrow, KV_W), BF16)],
        compiler_params=_cparams(("parallel",)),
        name="compress",
    )(r, *args)


def _heads_to_lanes(o):
    return jnp.concatenate([o[g * Q_TILE:(g + 1) * Q_TILE, :] for g in range(GROUP)], axis=1)


def _select_kernel(q_ref, kc_ref, vc_ref, ovl_ref, tri_ref, o_ref, bias_ref, *, nsb):
    s0 = pl.program_id(2) * Q_TILE
    rows = GROUP * Q_TILE
    ncp = kc_ref.shape[0]
    q = q_ref[...].reshape(rows, LANES)
    kc_hi, kc_mid, kc_lo = _split3(kc_ref[...])
    s = _dot_nt(q, kc_hi) + _dot_nt(q, kc_mid) + _dot_nt(q, kc_lo)
    t = s0 + (lax.broadcasted_iota(jnp.int32, (rows, ncp), 0) & (Q_TILE - 1))
    cend = lax.broadcasted_iota(jnp.int32, (rows, ncp), 1) * CMP_STRIDE + (CMP_BLOCK - 1)
    mask = cend <= t
    sm = jnp.where(mask, s, NEG)
    mx = jnp.max(sm, axis=1, keepdims=True)
    p = jnp.where(mask, jnp.exp(sm - mx), 0.0)
    l = jnp.sum(p, axis=1, keepdims=True)
    pc = p / jnp.maximum(l, 1e-30)
    o = _dot(pc.astype(BF16), vc_ref[...])
    o_ref[...] = _heads_to_lanes(o)
    psum = pc[0:Q_TILE]
    for g in range(1, GROUP):
        psum = psum + pc[g * Q_TILE:(g + 1) * Q_TILE]
    ovl = ovl_ref[...]
    imp = None
    for piece in _split3(psum):
        term = _dot(piece, ovl)
        imp = term if imp is None else imp + term
    tq = s0 + lax.broadcasted_iota(jnp.int32, (Q_TILE, nsb), 0)
    jb = lax.broadcasted_iota(jnp.int32, (Q_TILE, nsb), 1)
    cur = tq // SEL_BLOCK
    forced = (jb == 0) | (jb == cur) | (jb == cur - 1)
    val = jnp.where(jb <= cur, jnp.where(forced, jnp.inf, imp), -jnp.inf)
    vt = val.T
    tau = _topk_rows(vt, want_sorted=False)[0:1, :]
    gt = vt > tau
    tie = vt == tau
    need = float(N_SELECT) - jnp.sum(gt.astype(F32), axis=0, keepdims=True)
    pre = _dot(tri_ref[...], tie.astype(BF16))
    sel = (gt | (tie & (pre <= need))) & (vt > -1.0)
    bias_ref[...] = jnp.where(sel, 0.0, NEG).T.astype(bias_ref.dtype)


def _nsa_select(q4, kc4, vc4, bsz, seq):
    assert seq // SEL_BLOCK <= LANES, "block-selection kernel holds at most 128 selection blocks"
    nsb = LANES
    ncp = kc4.shape[2]
    ci = np.arange(ncp)[:, None] * CMP_STRIDE
    bj = np.arange(nsb)[None, :] * SEL_BLOCK
    ovl = ((ci < bj + SEL_BLOCK) & (ci + CMP_BLOCK > bj) & (np.arange(ncp)[:, None] < ncp - 1))
    ovl = jnp.asarray(ovl, BF16)
    tri = jnp.asarray(np.tril(np.ones((nsb, nsb))), BF16)
    full = lambda a: pl.BlockSpec(a.shape, lambda b, k, i: (0,) * a.ndim)
    return pl.pallas_call(
        functools.partial(_select_kernel, nsb=nsb),
        grid=(bsz, N_KV, seq // Q_TILE),
        in_specs=[pl.BlockSpec((None, None, GROUP, Q_TILE, LANES), lambda b, k, i: (b, k, 0, i, 0)),
                  pl.BlockSpec((None, None, ncp, LANES), lambda b, k, i: (b, k, 0, 0)),
                  pl.BlockSpec((None, None, ncp, HEAD_DIM), lambda b, k, i: (b, k, 0, 0)),
                  full(ovl), full(tri)],
        out_specs=[pl.BlockSpec((None, Q_TILE, GROUP * HEAD_DIM), lambda b, k, i: (b, i, k)),
                   pl.BlockSpec((None, None, Q_TILE, nsb), lambda b, k, i: (b, k, i, 0))],
        out_shape=[jax.ShapeDtypeStruct((bsz, seq, D_NSA), F32),
                   jax.ShapeDtypeStruct((bsz, N_KV, seq, nsb), BF16)],
        compiler_params=_cparams(("parallel", "parallel", "parallel")),
        name="nsa_select",
    )(q4, kc4, vc4, ovl, tri)


def _attend_kernel(q_ref, bias_ref, ka_ref, va_ref, kw_ref, vw_ref, os_ref, ow_ref):
    s0 = pl.program_id(2) * Q_TILE
    rows = GROUP * Q_TILE
    q = q_ref[...].reshape(rows, LANES)
    bias = bias_ref[...]
    qa = jnp.concatenate([q, jnp.concatenate([bias] * GROUP, axis=0)], axis=1)
    t_row = s0 + (lax.broadcasted_iota(jnp.int32, (rows, 1), 0) & (Q_TILE - 1))

    def step(kt, carry, masked):
        m, acc = carry
        k0 = pl.multiple_of(kt * K_TILE, K_TILE)
        s = _dot_nt(qa, ka_ref[pl.ds(k0, K_TILE), :])
        if masked:
            kpos = k0 + lax.broadcasted_iota(jnp.int32, (rows, K_TILE), 1)
            s = jnp.where(kpos <= t_row, s, NEG)
        mn = jnp.maximum(m, jnp.max(s, axis=1, keepdims=True))
        p = jnp.exp(s - mn)
        acc = jnp.exp(m - mn) * acc + _dot(p.astype(BF16), va_ref[pl.ds(k0, K_TILE), :])
        return mn, acc

    n_full = s0 // K_TILE
    init = (jnp.full((rows, 1), NEG, F32), jnp.zeros((rows, LANES), F32))
    carry = lax.fori_loop(0, n_full, lambda kt, c: step(kt, c, False), init)
    _, acc = step(n_full, carry, True)
    os_ref[...] = _heads_to_lanes(acc[:, :HEAD_DIM] / acc[:, HEAD_DIM:HEAD_DIM + 1])

    span = WINDOW + Q_TILE
    w0 = pl.multiple_of(jnp.maximum(s0 - WINDOW, 0), Q_TILE)
    s = _dot_nt(q, kw_ref[pl.ds(w0, span), :])
    kpos = w0 + lax.broadcasted_iota(jnp.int32, (rows, span), 1)
    mask = (kpos <= t_row) & (kpos > t_row - WINDOW)
    s = jnp.where(mask, s, NEG)
    p = jnp.where(mask, jnp.exp(s - jnp.max(s, axis=1, keepdims=True)), 0.0)
    acc = _dot(p.astype(BF16), vw_ref[pl.ds(w0, span), :])
    ow_ref[...] = _heads_to_lanes(acc[:, :HEAD_DIM] / acc[:, HEAD_DIM:HEAD_DIM + 1])


def _nsa_attend(q4, bias, ka, va, kw, vw, bsz, seq):
    per_head = lambda n: pl.BlockSpec((None, None, seq, n), lambda b, k, i: (b, k, 0, 0))
    out = pl.BlockSpec((None, Q_TILE, GROUP * HEAD_DIM), lambda b, k, i: (b, i, k))
    return pl.pallas_call(
        _attend_kernel,
        grid=(bsz, N_KV, seq // Q_TILE),
        in_specs=[pl.BlockSpec((None, None, GROUP, Q_TILE, LANES), lambda b, k, i: (b, k, 0, i, 0)),
                  pl.BlockSpec((None, None, Q_TILE, LANES), lambda b, k, i: (b, k, i, 0)),
                  per_head(2 * LANES), per_head(LANES), per_head(LANES), per_head(LANES)],
        out_specs=[out, out],
        out_shape=[jax.ShapeDtypeStruct((bsz, seq, D_NSA), F32)] * 2,
        compiler_params=_cparams(("parallel", "parallel", "arbitrary")),
        name="nsa_attend",
    )(q4, bias, ka, va, kw, vw)


def _memkv_kernel(m_ref, g_ref, wk_ref, wv_ref, k_ref, v_ref):
    mn = _rms(m_ref[...], g_ref[...]).astype(BF16)
    k_ref[...] = _dot(mn, wk_ref[...]).astype(k_ref.dtype)
    v_ref[...] = _dot(mn, wv_ref[...]).astype(v_ref.dtype)


def _mem_kv(mem, g, wk, wv):
    bsz, nm, _ = mem.shape
    g2 = g.reshape(1, D_MODEL)
    full = lambda a: pl.BlockSpec(a.shape, lambda b: (0,) * a.ndim)
    blk = pl.BlockSpec((None, nm, D_MODEL), lambda b: (b, 0, 0))
    return pl.pallas_call(
        _memkv_kernel,
        grid=(bsz,),
        in_specs=[blk, full(g2), full(wk), full(wv)],
        out_specs=[blk, blk],
        out_shape=[jax.ShapeDtypeStruct(mem.shape, BF16)] * 2,
        compiler_params=_cparams(("parallel",)),
        name="mem_kv",
    )(mem, g2, wk, wv)


def _mix_kernel(x_ref, ylru_ref, oc_ref, os_ref, ow_ref, gate_ref, e_ref, gnsa_ref, wout_ref,
                gq_ref, wq_ref, mk_ref, mv_ref, wo_ref, o_ref):
    g = gate_ref[...]
    g_hi = g.astype(BF16)
    g_lo = (g - g_hi.astype(F32)).astype(BF16)
    e = e_ref[...]
    gx = _dot(g_hi, e) + _dot(g_lo, e)
    y = (gx[:, 0:D_NSA] * oc_ref[...] + gx[:, D_NSA:2 * D_NSA] * os_ref[...]
         + gx[:, 2 * D_NSA:3 * D_NSA] * ow_ref[...])
    yn = _rms(y, gnsa_ref[...]).astype(BF16)
    ycat = jnp.concatenate([ylru_ref[...], yn], axis=1)
    x1 = x_ref[...] + _dot(ycat, wout_ref[...])
    xq = _rms(x1, gq_ref[...]).astype(BF16)
    qm = _dot(xq, wq_ref[...]).astype(BF16)
    outs = []
    for h in range(MEM_HEADS):
        sl = slice(h * MEM_HEAD_DIM, (h + 1) * MEM_HEAD_DIM)
        s = _dot_nt(qm[:, sl], mk_ref[:, sl])
        p = jnp.exp(s - jnp.max(s, axis=1, keepdims=True))
        p = p / jnp.sum(p, axis=1, keepdims=True)
        outs.append(_dot(p.astype(BF16), mv_ref[:, sl]))
    o = jnp.concatenate(outs, axis=1).astype(BF16)
    o_ref[...] = x1 + _dot(o, wo_ref[...])


def _mix_mem(x3, ylru, oc, osl, ow, gates, g_nsa, w_out, g_q, w_mq, mk, mv, w_mo):
    bsz, seq, _ = x3.shape
    tm = 256
    e = np.zeros((LANES, 3 * D_NSA), np.float32)
    for h in range(N_HEADS):
        for j in range(3):
            e[h * 3 + j, j * D_NSA + h * HEAD_DIM:j * D_NSA + (h + 1) * HEAD_DIM] = 1.0
    e = jnp.asarray(e, BF16)
    vec = lambda a: a.reshape(1, -1)
    wq = (w_mq * (MEM_HEAD_DIM ** -0.5)).astype(BF16)
    row = lambda n: pl.BlockSpec((None, tm, n), lambda b, i: (b, i, 0))
    full = lambda a: pl.BlockSpec(a.shape, lambda b, i: (0,) * a.ndim)
    memb = pl.BlockSpec((None, mk.shape[1], D_MODEL), lambda b, i: (b, 0, 0))
    consts = [e, vec(g_nsa), w_out.astype(BF16), vec(g_q), wq]
    return pl.pallas_call(
        _mix_kernel,
        grid=(bsz, seq // tm),
        in_specs=[row(D_MODEL), row(D_LRU), row(D_NSA), row(D_NSA), row(D_NSA), row(LANES)]
        + [full(a) for a in consts] + [memb, memb, full(w_mo)],
        out_specs=row(D_MODEL),
        out_shape=jax.ShapeDtypeStruct(x3.shape, F32),
        compiler_params=_cparams(("parallel", "parallel")),
        name="mix_mem",
    )(x3, ylru, oc, osl, ow, gates.reshape(bsz, seq, LANES), *consts, mk, mv, w_mo)


def _peer_pairs():
    k = PEER_TOPK
    return [(a, b) for a in range(k) for b in range(k) if (a + 1) * (b + 1) <= k]


def _route_kernel(x_ref, g_ref, wpq_ref, sk_ref, xn_ref, r2_ref, e2_ref, ns_ref, sc_ref,
                  q_scr, s_scr, l_scr, t_scr, *, tt):
    k = PEER_TOPK
    nhc = 2 * PEER_HEADS
    xn = _rms(x_ref[...], g_ref[...]).astype(BF16)
    xn_ref[...] = xn
    q = _dot(xn, wpq_ref[...]).astype(BF16)
    for hc in range(nhc):
        q_scr[hc] = q[:, hc * PEER_NKEYS:(hc + 1) * PEER_NKEYS]

    def score_and_sort(hc, carry):
        st = _dot_nt(sk_ref[hc], q_scr[hc])
        s_scr[hc] = st
        rows = _topk_rows(st, want_sorted=True)
        h, c = hc // 2, hc % 2
        for a in range(k):
            l_scr[c, a, pl.ds(h, 1), :] = rows[a][0:1, :]
        return carry

    lax.fori_loop(0, nhc, score_and_sort, 0)
    v1 = [l_scr[0, a] for a in range(k)]
    v2 = [l_scr[1, b] for b in range(k)]
    pairs = _peer_pairs()
    cand = [v1[a] + v2[b] for a, b in pairs]
    pad = [jnp.full_like(cand[0], -jnp.inf)] * (4 * k - len(cand))
    groups = [_sort_desc((cand + pad)[i * k:(i + 1) * k]) for i in range(4)]
    left = _bitonic_merge_desc(_top_half(groups[0], groups[1]))
    right = _bitonic_merge_desc(_top_half(groups[2], groups[3]))
    tau = functools.reduce(jnp.minimum, _top_half(left, right))
    top = v1[0] + v2[0]
    z = jnp.zeros_like(top)
    for c_ in cand:
        z = z + jnp.where(c_ >= tau, jnp.exp(c_ - top), 0.0)
    t_scr[0] = tau
    t_scr[1] = 1.0 / z

    def per_head(h, carry):
        s1, s2 = s_scr[2 * h], s_scr[2 * h + 1]
        tau_h = t_scr[0, pl.ds(h, 1), :]
        zinv_h = t_scr[1, pl.ds(h, 1), :]
        rank = jnp.zeros((PEER_NKEYS, tt), F32)
        cnt = jnp.zeros((PEER_NKEYS, tt), F32)
        for b in range(k):
            lb = l_scr[1, b, pl.ds(h, 1), :]
            rank = rank + (lb > s2).astype(F32)
            cnt = cnt + ((s1 + lb) >= tau_h).astype(F32)
        r2_ref[h] = rank.astype(r2_ref.dtype)
        ns_ref[h] = cnt
        e2_ref[h] = jnp.exp(s2 - l_scr[1, 0, pl.ds(h, 1), :]).astype(e2_ref.dtype)
        sc_ref[h] = jnp.exp(s1 - l_scr[0, 0, pl.ds(h, 1), :]) * zinv_h
        return carry

    lax.fori_loop(0, PEER_HEADS, per_head, 0)


def _peer_route(x2d, g_ffn, w_pq, sub_keys):
    m = x2d.shape[0]
    tt = min(512, m)
    g = g_ffn.reshape(1, D_MODEL)
    wpq = w_pq.astype(BF16)
    sk = sub_keys.astype(BF16).reshape(2 * PEER_HEADS, PEER_NKEYS, PEER_QDIM // 2)
    full = lambda a: pl.BlockSpec(a.shape, lambda i: (0,) * a.ndim)
    tcol = pl.BlockSpec((PEER_HEADS, PEER_NKEYS, tt), lambda i: (0, 0, i))
    shp = lambda dt: jax.ShapeDtypeStruct((PEER_HEADS, PEER_NKEYS, m), dt)
    return pl.pallas_call(
        functools.partial(_route_kernel, tt=tt),
        grid=(m // tt,),
        in_specs=[pl.BlockSpec((tt, D_MODEL), lambda i: (i, 0)), full(g), full(wpq), full(sk)],
        out_specs=[pl.BlockSpec((tt, D_MODEL), lambda i: (i, 0)), tcol, tcol, tcol, tcol],
        out_shape=[jax.ShapeDtypeStruct((m, D_MODEL), BF16), shp(BF16), shp(BF16), shp(F32), shp(F32)],
        scratch_shapes=[pltpu.VMEM((2 * PEER_HEADS, tt, PEER_NKEYS), BF16),
                        pltpu.VMEM((2 * PEER_HEADS, PEER_NKEYS, tt), F32),
                        pltpu.VMEM((2, PEER_TOPK, PEER_HEADS, tt), F32),
                        pltpu.VMEM((2, PEER_HEADS, tt), F32)],
        compiler_params=_cparams(("parallel",)),
        name="peer_route",
    )(x2d, g, wpq, sk)


def _dense_kernel(x_ref, xn_ref, r2_ref, e2_ref, ns_ref, sc_ref, u_ref, vt_ref, gf_ref,
                  o_ref, acc_ref, a_ref, *, i1b, tt):
    ei = pl.program_id(1)

    @pl.when(ei == 0)
    def _():
        acc_ref[...] = jnp.zeros_like(acc_ref)

    gh = _gelu(_dot_nt(u_ref[...], xn_ref[...]).astype(BF16))
    for kk in range(i1b):
        i1 = ei * i1b + kk
        w = None
        for h in range(PEER_HEADS):
            n_row = ns_ref[h, pl.ds(i1, 1), :].astype(BF16)
            s_row = sc_ref[h, pl.ds(i1, 1), :].astype(BF16)
            term = jnp.where(r2_ref[h] < n_row, e2_ref[h], jnp.zeros((), BF16)) * s_row
            w = term if w is None else w + term
        rs = slice(kk * PEER_NKEYS, (kk + 1) * PEER_NKEYS)
        a_ref[rs, :] = w * gh[rs, :]
    acc_ref[...] += _dot(vt_ref[...], a_ref[...])

    @pl.when(ei == pl.num_programs(1) - 1)
    def _():
        o_ref[...] = _rms(x_ref[...] + acc_ref[...].T, gf_ref[...])


def _peer_dense(x2d, xn, r2, e2, ns, sc, peer_u, peer_v, g_final):
    m = x2d.shape[0]
    tt = min(512, m)
    i1b = 8
    te = i1b * PEER_NKEYS
    u = peer_u.astype(BF16)
    vt = peer_v.T.astype(BF16)
    gf = g_final.reshape(1, D_MODEL)
    tok = pl.BlockSpec((tt, D_MODEL), lambda t, e: (t, 0))
    tcol = pl.BlockSpec((PEER_HEADS, PEER_NKEYS, tt), lambda t, e: (0, 0, t))
    return pl.pallas_call(
        functools.partial(_dense_kernel, i1b=i1b, tt=tt),
        grid=(m // tt, PEER_NKEYS // i1b),
        in_specs=[tok, tok, tcol, tcol, tcol, tcol,
                  pl.BlockSpec((te, D_MODEL), lambda t, e: (e, 0)),
                  pl.BlockSpec((D_MODEL, te), lambda t, e: (0, e)),
                  pl.BlockSpec((1, D_MODEL), lambda t, e: (0, 0))],
        out_specs=tok,
        out_shape=jax.ShapeDtypeStruct((m, D_MODEL), F32),
        scratch_shapes=[pltpu.VMEM((D_MODEL, tt), F32), pltpu.VMEM((te, tt), BF16)],
        compiler_params=_cparams(("parallel", "arbitrary")),
        name="peer_dense",
    )(x2d, xn, r2, e2, ns, sc, u, vt, gf)


def _per_kv_head(a, bsz, seq):
    return a.reshape(bsz, seq, N_KV, HEAD_DIM).transpose(0, 2, 1, 3)


def _layer(x, mem, p):
    bsz, seq, _ = x.shape
    m = bsz * seq
    xl, gl, q, kvc, ks, vs, kw, vw, gates = _in_proj(x.reshape(m, D_MODEL), p["g_mix"], p["w_in"],
                                                     p["b_gate"])
    ylru = _rg_lru(xl, gl, p["conv_w"], p["conv_b"], p["w_rg_a"], p["b_rg_a"], p["w_rg_i"],
                   p["b_rg_i"], p["lam"], p["g_out_lru"], bsz, seq)
    kc, vc = _compress(kvc, p["cmp_pos_k"], p["cmp_w1_k"], p["cmp_w2_k"], p["cmp_pos_v"],
                       p["cmp_w1_v"], p["cmp_w2_v"], bsz, seq)
    ncp = seq // CMP_STRIDE
    pad64 = lambda a: jnp.pad(a, ((0, 0),) * (a.ndim - 1) + ((0, LANES - HEAD_DIM),))
    kc4 = pad64(kc.reshape(bsz, ncp, N_KV, HEAD_DIM).transpose(0, 2, 1, 3))
    vc4 = vc.reshape(bsz, ncp, N_KV, HEAD_DIM).transpose(0, 2, 1, 3)
    q4 = pad64(q.reshape(bsz, seq, N_KV, GROUP, HEAD_DIM).transpose(0, 2, 3, 1, 4))
    o_cmp, bias = _nsa_select(q4, kc4, vc4, bsz, seq)

    nsb = LANES
    onehot = jnp.asarray(np.arange(seq)[:, None] // SEL_BLOCK == np.arange(nsb)[None, :], BF16)
    ones_col = jnp.asarray(np.arange(LANES - HEAD_DIM)[None, :] == 0, BF16)

    def with_ones(v):
        v4 = _per_kv_head(v, bsz, seq)
        return jnp.concatenate([v4, jnp.broadcast_to(ones_col, v4.shape[:3] + (LANES - HEAD_DIM,))], -1)

    ka = jnp.concatenate([pad64(_per_kv_head(ks, bsz, seq)),
                          jnp.broadcast_to(onehot, (bsz, N_KV, seq, nsb))], axis=-1)
    o_slc, o_win = _nsa_attend(q4, bias, ka, with_ones(vs), pad64(_per_kv_head(kw, bsz, seq)),
                               with_ones(vw), bsz, seq)

    mk, mv = _mem_kv(mem, p["g_mem_kv"], p["w_mk"].astype(BF16), p["w_mv"].astype(BF16))
    x2 = _mix_mem(x, ylru, o_cmp, o_slc, o_win, gates, p["g_out_nsa"], p["w_out"], p["g_mem_q"],
                  p["w_mq"], mk, mv, p["w_mo"].astype(BF16))
    return x2.reshape(m, D_MODEL)


def kernel(x, mem, g_mix, w_in, b_gate, conv_w, conv_b, w_rg_a, b_rg_a, w_rg_i, b_rg_i, lam,
           cmp_pos_k, cmp_w1_k, cmp_w2_k, cmp_pos_v, cmp_w1_v, cmp_w2_v, g_out_lru, g_out_nsa,
           w_out, g_mem_q, g_mem_kv, w_mq, w_mk, w_mv, w_mo, g_ffn, w_pq, sub_keys, peer_u,
           peer_v, g_final):
    depth = w_in.shape[0]
    assert depth == 1, "the final norm is fused into the single layer's PEER kernel"
    names = ["g_mix", "w_in", "b_gate", "conv_w", "conv_b", "w_rg_a", "b_rg_a", "w_rg_i", "b_rg_i",
             "lam", "cmp_pos_k", "cmp_w1_k", "cmp_w2_k", "cmp_pos_v", "cmp_w1_v", "cmp_w2_v",
             "g_out_lru", "g_out_nsa", "w_out", "g_mem_q", "g_mem_kv", "w_mq", "w_mk", "w_mv", "w_mo",
             "g_ffn", "w_pq", "sub_keys", "peer_u", "peer_v"]
    vals = [g_mix, w_in, b_gate, conv_w, conv_b, w_rg_a, b_rg_a, w_rg_i, b_rg_i, lam, cmp_pos_k,
            cmp_w1_k, cmp_w2_k, cmp_pos_v, cmp_w1_v, cmp_w2_v, g_out_lru, g_out_nsa, w_out, g_mem_q,
            g_mem_kv, w_mq, w_mk, w_mv, w_mo, g_ffn, w_pq, sub_keys, peer_u, peer_v]
    p = {n: v[0] for n, v in zip(names, vals)}
    bsz, seq, _ = x.shape
    x2 = _layer(x, mem, p)
    xn, r2, e2, ns, sc = _peer_route(x2, p["g_ffn"], p["w_pq"], p["sub_keys"])
    out = _peer_dense(x2, xn, r2, e2, ns, sc, p["peer_u"], p["peer_v"], g_final)
    return out.reshape(bsz, seq, D_MODEL)
```

```python
import functools

import numpy as np
import jax
import jax.numpy as jnp
from jax import lax
from jax.experimental import pallas as pl
from jax.experimental.pallas import tpu as pltpu

F32 = jnp.float32
BF16 = jnp.bfloat16

D_MODEL = 1024
N_MEM = 256
D_LRU = 512
LRU_BLOCKS = 8
CONV_W = 4
LRU_C = 8.0
N_HEADS = 8
N_KV = 2
GROUP = N_HEADS // N_KV
HEAD_DIM = 64
D_NSA = N_HEADS * HEAD_DIM
CMP_BLOCK = 32
CMP_STRIDE = 16
SEL_BLOCK = 64
N_SELECT = 16
WINDOW = 512
MEM_HEADS = 4
MEM_HEAD_DIM = D_MODEL // MEM_HEADS
PEER_HEADS = 8
PEER_NKEYS = 128
PEER_TOPK = 16
PEER_QDIM = 256
EPS = 1e-6
KV_W = N_KV * HEAD_DIM
N_IN = 2 * D_LRU + D_NSA + 6 * KV_W + 3 * N_HEADS
N_IN_PAD = 2 * D_LRU + D_NSA + 6 * KV_W + 128

LANES = 128
SUBLANES = 8
NEG = -1e30
VMEM_LIMIT = 56 * 1024 * 1024

Q_TILE = 256
K_TILE = 512


def _cparams(sem):
    return pltpu.CompilerParams(dimension_semantics=sem, vmem_limit_bytes=VMEM_LIMIT)


def _rms(x, g):
    return x * lax.rsqrt(jnp.mean(x * x, axis=-1, keepdims=True) + EPS) * g


def _gelu(x):
    c = 0.7978845608028654
    return 0.5 * x * (1.0 + jnp.tanh(c * (x + 0.044715 * (x * x * x))))


def _dot(a, b):
    return jnp.dot(a, b, preferred_element_type=F32)


def _dot_nt(a, b):
    return lax.dot_general(a, b, (((1,), (1,)), ((), ())), preferred_element_type=F32)


def _split3(a):
    hi = a.astype(BF16)
    r1 = a - hi.astype(F32)
    mid = r1.astype(BF16)
    lo = (r1 - mid.astype(F32)).astype(BF16)
    return hi, mid, lo


def _batcher_pairs(n):
    pairs = []
    p = 1
    while p < n:
        k = p
        while k >= 1:
            for j in range(k % p, n - k, 2 * k):
                for i in range(min(k, n - j - k)):
                    if (i + j) // (2 * p) == (i + j + k) // (2 * p):
                        pairs.append((i + j, i + j + k))
            k //= 2
        p *= 2
    return pairs


_SORT16 = _batcher_pairs(16)


def _sort_desc(xs):
    xs = list(xs)
    for i, j in _batcher_pairs(len(xs)):
        a, b = xs[i], xs[j]
        xs[i], xs[j] = jnp.maximum(a, b), jnp.minimum(a, b)
    return xs


def _bitonic_merge_desc(xs):
    xs = list(xs)
    n = len(xs)
    stride = n // 2
    while stride >= 1:
        for i in range(n):
            if (i & stride) == 0:
                a, b = xs[i], xs[i + stride]
                xs[i], xs[i + stride] = jnp.maximum(a, b), jnp.minimum(a, b)
        stride //= 2
    return xs


def _top_half(a, b):
    n = len(a)
    return [jnp.maximum(a[i], b[n - 1 - i]) for i in range(n)]


def _topk_rows(x, *, want_sorted):
    k = PEER_TOPK
    rows = [x[SUBLANES * v:SUBLANES * (v + 1), :] for v in range(k)]
    rows = _sort_desc(rows)
    for step, shift in enumerate((4, 2, 1)):
        part = [pltpu.roll(r, shift, 0) for r in rows]
        rows = _top_half(rows, part)
        if step < 2 or want_sorted:
            rows = _bitonic_merge_desc(rows)
    if want_sorted:
        return rows
    return functools.reduce(jnp.minimum, rows)


def _inproj_kernel(x_ref, g_ref, w_ref, bg_ref,
                   xl_ref, gl_ref, q_ref, kvc_ref, ks_ref, vs_ref, kw_ref, vw_ref, gate_ref):
    h = _rms(x_ref[...], g_ref[...]).astype(BF16)
    z = _dot(h, w_ref[...])
    o = 0
    xl_ref[...] = z[:, o:o + D_LRU]; o += D_LRU
    gl_ref[...] = z[:, o:o + D_LRU]; o += D_LRU
    q_ref[...] = (z[:, o:o + D_NSA] * (HEAD_DIM ** -0.5)).astype(BF16); o += D_NSA
    kvc_ref[...] = z[:, o:o + 2 * KV_W]; o += 2 * KV_W
    ks_ref[...] = z[:, o:o + KV_W].astype(BF16); o += KV_W
    vs_ref[...] = z[:, o:o + KV_W].astype(BF16); o += KV_W
    kw_ref[...] = z[:, o:o + KV_W].astype(BF16); o += KV_W
    vw_ref[...] = z[:, o:o + KV_W].astype(BF16); o += KV_W
    gate_ref[...] = jax.nn.sigmoid(z[:, o:o + LANES] + bg_ref[...])


def _in_proj(x2d, g_mix, w_in, b_gate):
    m = x2d.shape[0]
    tm = 512
    w = jnp.pad(w_in, ((0, 0), (0, N_IN_PAD - N_IN))).astype(BF16)
    bg = jnp.pad(b_gate, (0, LANES - 3 * N_HEADS)).reshape(1, LANES)
    row = lambda n: pl.BlockSpec((tm, n), lambda i: (i, 0))
    full = lambda a: pl.BlockSpec(a.shape, lambda i: (0,) * a.ndim)
    g = g_mix.reshape(1, D_MODEL)
    outs = [(D_LRU, F32), (D_LRU, F32), (D_NSA, BF16), (2 * KV_W, F32), (KV_W, BF16), (KV_W, BF16),
            (KV_W, BF16), (KV_W, BF16), (LANES, F32)]
    return pl.pallas_call(
        _inproj_kernel,
        grid=(m // tm,),
        in_specs=[row(D_MODEL), full(g), full(w), full(bg)],
        out_specs=[row(n) for n, _ in outs],
        out_shape=[jax.ShapeDtypeStruct((m, n), dt) for n, dt in outs],
        compiler_params=_cparams(("parallel",)),
        name="in_proj",
    )(x2d, g, w, bg)


def _rglru_kernel(x_ref, xprev_ref, gate_ref, cw_ref, cb_ref, wa_ref, ba_ref, wi_ref, bi_ref,
                  lam_ref, gout_ref, o_ref, h_ref, *, t):
    i = pl.program_id(1)

    @pl.when(i == 0)
    def _():
        h_ref[...] = jnp.zeros_like(h_ref)

    x = x_ref[...]
    prev = jnp.where(i > 0, xprev_ref[...], 0.0)
    xs = jnp.concatenate([prev, x], axis=0)
    cw = cw_ref[...]
    xc = cb_ref[...] + x * cw[CONV_W - 1:CONV_W, :]
    for k in range(1, CONV_W):
        xk = pltpu.roll(xs, k, 0)[SUBLANES:, :]
        xc = xc + xk * cw[CONV_W - 1 - k:CONV_W - k, :]
    xcb = xc.astype(BF16)
    r = jax.nn.sigmoid(_dot(xcb, wa_ref[...]) + ba_ref[...])
    gi = jax.nn.sigmoid(_dot(xcb, wi_ref[...]) + bi_ref[...])
    lam = lam_ref[...]
    softplus = jnp.maximum(-lam, 0.0) + jnp.log1p(jnp.exp(-jnp.abs(lam)))
    log_a = (-LRU_C) * r * softplus
    a = jnp.exp(log_a)
    b = jnp.sqrt(-jnp.tanh(log_a) * (a * a + 1.0)) * (gi * xc)
    rowi = lax.broadcasted_iota(jnp.int32, (t, D_LRU), 0)
    d = 1
    while d < t:
        ok = rowi >= d
        a_s = pltpu.roll(a, d, 0)
        b_s = pltpu.roll(b, d, 0)
        b = jnp.where(ok, a * b_s, 0.0) + b
        a = jnp.where(ok, a * a_s, a)
        d *= 2
    h = b + a * h_ref[0:1, :]
    h_ref[...] = jnp.broadcast_to(h[t - 1:t, :], h_ref.shape)
    y = h * _gelu(gate_ref[...])
    o_ref[...] = _rms(y, gout_ref[...]).astype(o_ref.dtype)


def _block_diag(w):
    nb, bd, _ = w.shape
    eye = jnp.eye(nb, dtype=w.dtype)
    return (eye[:, None, :, None] * w[:, :, None, :]).reshape(nb * bd, nb * bd)


def _rg_lru(xl, gl, conv_w, conv_b, w_a, b_a, w_i, b_i, lam, g_out, bsz, seq):
    t = 256
    xl3 = xl.reshape(bsz, seq, D_LRU)
    gl3 = gl.reshape(bsz, seq, D_LRU)
    cw = jnp.pad(conv_w, ((0, SUBLANES - CONV_W), (0, 0)))
    vec = lambda a: a.reshape(1, D_LRU)
    wa = _block_diag(w_a).astype(BF16)
    wi = _block_diag(w_i).astype(BF16)
    args = [cw, vec(conv_b), wa, vec(b_a), wi, vec(b_i), vec(lam), vec(g_out)]
    full = lambda a: pl.BlockSpec(a.shape, lambda b, i: (0,) * a.ndim)
    tile = pl.BlockSpec((None, t, D_LRU), lambda b, i: (b, i, 0))
    prev = pl.BlockSpec((None, SUBLANES, D_LRU),
                        lambda b, i: (b, jnp.maximum(i * (t // SUBLANES) - 1, 0), 0))
    return pl.pallas_call(
        functools.partial(_rglru_kernel, t=t),
        grid=(bsz, seq // t),
        in_specs=[tile, prev, tile] + [full(a) for a in args],
        out_specs=tile,
        out_shape=jax.ShapeDtypeStruct((bsz, seq, D_LRU), BF16),
        scratch_shapes=[pltpu.VMEM((SUBLANES, D_LRU), F32)],
        compiler_params=_cparams(("parallel", "arbitrary")),
        name="rg_lru",
    )(xl3, xl3, gl3, *args)


def _compress_kernel(r_ref, pa_ref, pb_ref, w1a_ref, w1b_ref, w2_ref, k_ref, v_ref):
    r = r_ref[...]
    n = r.shape[0]

    def dot3(x, w_ref):
        out = None
        for piece in _split3(x):
            for wi in range(3):
                term = _dot(piece, w_ref[wi])
                out = term if out is None else out + term
        return out

    first = dot3(r + pa_ref[...], w1a_ref)
    second = dot3(r + pb_ref[...], w1b_ref)
    pre = first + pltpu.roll(second, n - 1, 0)
    out = dot3(_gelu(pre), w2_ref)
    k_ref[...] = out[:, :KV_W]
    v_ref[...] = out[:, KV_W:].astype(v_ref.dtype)


def _compress(kvc, pos_k, w1_k, w2_k, pos_v, w1_v, w2_v, bsz, seq):
    nrow = seq // CMP_STRIDE
    width = CMP_STRIDE * 2 * KV_W
    r = kvc.reshape(bsz, nrow, width)
    eye = jnp.eye(N_KV, dtype=F32)

    def big1(w1, half):
        w = w1.reshape(CMP_BLOCK, HEAD_DIM, HEAD_DIM)[half * CMP_STRIDE:(half + 1) * CMP_STRIDE]
        return jnp.einsum("ldo,hg->lhdgo", w, eye).reshape(CMP_STRIDE, KV_W, KV_W)

    def both1(half):
        z = jnp.zeros((CMP_STRIDE, KV_W, KV_W), F32)
        top = jnp.concatenate([big1(w1_k, half), z], axis=2)
        bot = jnp.concatenate([z, big1(w1_v, half)], axis=2)
        return jnp.concatenate([top, bot], axis=1).reshape(width, 2 * KV_W)

    def pos_row(half):
        pk = jnp.tile(pos_k[half * CMP_STRIDE:(half + 1) * CMP_STRIDE, None, :], (1, N_KV, 1))
        pv = jnp.tile(pos_v[half * CMP_STRIDE:(half + 1) * CMP_STRIDE, None, :], (1, N_KV, 1))
        return jnp.concatenate([pk, pv], axis=1).reshape(1, width)

    def w2big():
        zk = jnp.zeros((KV_W, KV_W), F32)
        k2 = jnp.einsum("do,hg->hdgo", w2_k, eye).reshape(KV_W, KV_W)
        v2 = jnp.einsum("do,hg->hdgo", w2_v, eye).reshape(KV_W, KV_W)
        return jnp.concatenate([jnp.concatenate([k2, zk], 1), jnp.concatenate([zk, v2], 1)], 0)

    pieces = lambda w: jnp.stack(_split3(w))
    args = [pos_row(0), pos_row(1), pieces(both1(0)), pieces(both1(1)), pieces(w2big())]
    full = lambda a: pl.BlockSpec(a.shape, lambda b: (0,) * a.ndim)
    return pl.pallas_call(
        _compress_kernel,
        grid=(bsz,),
        in_specs=[pl.BlockSpec((None, nrow, width), lambda b: (b, 0, 0))] + [full(a) for a in args],
        out_specs=[pl.BlockSpec((None, nrow, KV_W), lambda b: (b, 0, 0))] * 2,
        out_shape=[jax.ShapeDtypeStruct((bsz, nrow, KV_W), F32),
                   jax.ShapeDtypeStruct((bsz, nrow, KV_W), BF16)],
        compiler_params=_cparams(("parallel",)),
        name="compress",
    )(r, *args)


def _heads_to_lanes(o):
    return jnp.concatenate([o[g * Q_TILE:(g + 1) * Q_TILE, :] for g in range(GROUP)], axis=1)


def _select_kernel(q_ref, kc_ref, vc_ref, ovl_ref, tri_ref, o_ref, bias_ref, *, nsb):
    s0 = pl.program_id(2) * Q_TILE
    rows = GROUP * Q_TILE
    ncp = kc_ref.shape[1]
    q = q_ref[...].reshape(rows, LANES)
    kc_hi, kc_mid, kc_lo = _split3(kc_ref[...])
    s = _dot(q, kc_hi) + _dot(q, kc_mid) + _dot(q, kc_lo)
    t = s0 + (lax.broadcasted_iota(jnp.int32, (rows, ncp), 0) & (Q_TILE - 1))
    cend = lax.broadcasted_iota(jnp.int32, (rows, ncp), 1) * CMP_STRIDE + (CMP_BLOCK - 1)
    mask = cend <= t
    sm = jnp.where(mask, s, NEG)
    mx = jnp.max(sm, axis=1, keepdims=True)
    p = jnp.where(mask, jnp.exp(sm - mx), 0.0)
    l = jnp.sum(p, axis=1, keepdims=True)
    pc = p / jnp.maximum(l, 1e-30)
    o = _dot(pc.astype(BF16), vc_ref[...])
    o_ref[...] = _heads_to_lanes(o)
    psum = pc[0:Q_TILE]
    for g in range(1, GROUP):
        psum = psum + pc[g * Q_TILE:(g + 1) * Q_TILE]
    ovl = ovl_ref[...]
    imp = None
    for piece in _split3(psum):
        term = _dot(piece, ovl)
        imp = term if imp is None else imp + term
    tq = s0 + lax.broadcasted_iota(jnp.int32, (Q_TILE, nsb), 0)
    jb = lax.broadcasted_iota(jnp.int32, (Q_TILE, nsb), 1)
    cur = tq // SEL_BLOCK
    forced = (jb == 0) | (jb == cur) | (jb == cur - 1)
    val = jnp.where(jb <= cur, jnp.where(forced, jnp.inf, imp), -jnp.inf)
    vt = val.T
    tau = _topk_rows(vt, want_sorted=False)[0:1, :]
    gt = vt > tau
    tie = vt == tau
    need = float(N_SELECT) - jnp.sum(gt.astype(F32), axis=0, keepdims=True)
    pre = _dot(tri_ref[...], tie.astype(BF16))
    sel = (gt | (tie & (pre <= need))) & (vt > -1.0)
    bias_ref[...] = jnp.where(sel, 0.0, NEG).T.astype(bias_ref.dtype)


def _nsa_select(q4, kc4, vc4, bsz, seq):
    assert seq // SEL_BLOCK <= LANES, "block-selection kernel holds at most 128 selection blocks"
    nsb = LANES
    ncp = kc4.shape[3]
    ci = np.arange(ncp)[:, None] * CMP_STRIDE
    bj = np.arange(nsb)[None, :] * SEL_BLOCK
    ovl = ((ci < bj + SEL_BLOCK) & (ci + CMP_BLOCK > bj) & (np.arange(ncp)[:, None] < ncp - 1))
    ovl = jnp.asarray(ovl, BF16)
    tri = jnp.asarray(np.tril(np.ones((nsb, nsb))), BF16)
    full = lambda a: pl.BlockSpec(a.shape, lambda b, k, i: (0,) * a.ndim)
    return pl.pallas_call(
        functools.partial(_select_kernel, nsb=nsb),
        grid=(bsz, N_KV, seq // Q_TILE),
        in_specs=[pl.BlockSpec((None, None, GROUP, Q_TILE, LANES), lambda b, k, i: (b, k, 0, i, 0)),
                  pl.BlockSpec((None, None, LANES, ncp), lambda b, k, i: (b, k, 0, 0)),
                  pl.BlockSpec((None, None, ncp, HEAD_DIM), lambda b, k, i: (b, k, 0, 0)),
                  full(ovl), full(tri)],
        out_specs=[pl.BlockSpec((None, Q_TILE, GROUP * HEAD_DIM), lambda b, k, i: (b, i, k)),
                   pl.BlockSpec((None, None, Q_TILE, nsb), lambda b, k, i: (b, k, i, 0))],
        out_shape=[jax.ShapeDtypeStruct((bsz, seq, D_NSA), F32),
                   jax.ShapeDtypeStruct((bsz, N_KV, seq, nsb), BF16)],
        compiler_params=_cparams(("parallel", "parallel", "parallel")),
        name="nsa_select",
    )(q4, kc4, vc4, ovl, tri)


def _attend_kernel(q_ref, bias_ref, ka_ref, va_ref, kw_ref, vw_ref, os_ref, ow_ref):
    s0 = pl.program_id(2) * Q_TILE
    rows = GROUP * Q_TILE
    q = q_ref[...].reshape(rows, LANES)
    bias = bias_ref[...]
    qa = jnp.concatenate([q, jnp.concatenate([bias] * GROUP, axis=0)], axis=1)
    t_row = s0 + (lax.broadcasted_iota(jnp.int32, (rows, 1), 0) & (Q_TILE - 1))

    def step(kt, carry, masked):
        m, acc = carry
        k0 = pl.multiple_of(kt * K_TILE, K_TILE)
        s = _dot(qa, ka_ref[:, pl.ds(k0, K_TILE)])
        if masked:
            kpos = k0 + lax.broadcasted_iota(jnp.int32, (rows, K_TILE), 1)
            s = jnp.where(kpos <= t_row, s, NEG)
        mn = jnp.maximum(m, jnp.max(s, axis=1, keepdims=True))
        p = jnp.exp(s - mn)
        acc = jnp.exp(m - mn) * acc + _dot(p.astype(BF16), va_ref[pl.ds(k0, K_TILE), :])
        return mn, acc

    n_full = s0 // K_TILE
    init = (jnp.full((rows, 1), NEG, F32), jnp.zeros((rows, LANES), F32))
    carry = lax.fori_loop(0, n_full, lambda kt, c: step(kt, c, False), init)
    _, acc = step(n_full, carry, True)
    os_ref[...] = _heads_to_lanes(acc[:, :HEAD_DIM] / acc[:, HEAD_DIM:HEAD_DIM + 1])

    span = WINDOW + Q_TILE
    w0 = pl.multiple_of(jnp.maximum(s0 - WINDOW, 0), Q_TILE)
    s = _dot(q, kw_ref[:, pl.ds(w0, span)])
    kpos = w0 + lax.broadcasted_iota(jnp.int32, (rows, span), 1)
    mask = (kpos <= t_row) & (kpos > t_row - WINDOW)
    s = jnp.where(mask, s, NEG)
    p = jnp.where(mask, jnp.exp(s - jnp.max(s, axis=1, keepdims=True)), 0.0)
    acc = _dot(p.astype(BF16), vw_ref[pl.ds(w0, span), :])
    ow_ref[...] = _heads_to_lanes(acc[:, :HEAD_DIM] / acc[:, HEAD_DIM:HEAD_DIM + 1])


def _nsa_attend(q4, bias, ka, va, kw, vw, bsz, seq):
    per_head = lambda n: pl.BlockSpec((None, None, seq, n), lambda b, k, i: (b, k, 0, 0))
    per_head_t = lambda n: pl.BlockSpec((None, None, n, seq), lambda b, k, i: (b, k, 0, 0))
    out = pl.BlockSpec((None, Q_TILE, GROUP * HEAD_DIM), lambda b, k, i: (b, i, k))
    return pl.pallas_call(
        _attend_kernel,
        grid=(bsz, N_KV, seq // Q_TILE),
        in_specs=[pl.BlockSpec((None, None, GROUP, Q_TILE, LANES), lambda b, k, i: (b, k, 0, i, 0)),
                  pl.BlockSpec((None, None, Q_TILE, LANES), lambda b, k, i: (b, k, i, 0)),
                  per_head_t(2 * LANES), per_head(LANES), per_head_t(LANES), per_head(LANES)],
        out_specs=[out, out],
        out_shape=[jax.ShapeDtypeStruct((bsz, seq, D_NSA), F32)] * 2,
        compiler_params=_cparams(("parallel", "parallel", "arbitrary")),
        name="nsa_attend",
    )(q4, bias, ka, va, kw, vw)


def _memkv_kernel(m_ref, g_ref, wk_ref, wv_ref, k_ref, v_ref):
    mn = _rms(m_ref[...], g_ref[...]).astype(BF16)
    k_ref[...] = _dot(mn, wk_ref[...]).astype(k_ref.dtype)
    v_ref[...] = _dot(mn, wv_ref[...]).astype(v_ref.dtype)


def _mem_kv(mem, g, wk, wv):
    bsz, nm, _ = mem.shape
    g2 = g.reshape(1, D_MODEL)
    full = lambda a: pl.BlockSpec(a.shape, lambda b: (0,) * a.ndim)
    blk = pl.BlockSpec((None, nm, D_MODEL), lambda b: (b, 0, 0))
    return pl.pallas_call(
        _memkv_kernel,
        grid=(bsz,),
        in_specs=[blk, full(g2), full(wk), full(wv)],
        out_specs=[blk, blk],
        out_shape=[jax.ShapeDtypeStruct(mem.shape, BF16)] * 2,
        compiler_params=_cparams(("parallel",)),
        name="mem_kv",
    )(mem, g2, wk, wv)


def _mix_kernel(x_ref, ylru_ref, oc_ref, os_ref, ow_ref, gate_ref, e_ref, gnsa_ref, wout_ref,
                gq_ref, wq_ref, mk_ref, mv_ref, wo_ref, o_ref):
    g = gate_ref[...]
    g_hi = g.astype(BF16)
    g_lo = (g - g_hi.astype(F32)).astype(BF16)
    e = e_ref[...]
    gx = _dot(g_hi, e) + _dot(g_lo, e)
    y = (gx[:, 0:D_NSA] * oc_ref[...] + gx[:, D_NSA:2 * D_NSA] * os_ref[...]
         + gx[:, 2 * D_NSA:3 * D_NSA] * ow_ref[...])
    yn = _rms(y, gnsa_ref[...]).astype(BF16)
    ycat = jnp.concatenate([ylru_ref[...], yn], axis=1)
    x1 = x_ref[...] + _dot(ycat, wout_ref[...])
    xq = _rms(x1, gq_ref[...]).astype(BF16)
    qm = _dot(xq, wq_ref[...]).astype(BF16)
    outs = []
    for h in range(MEM_HEADS):
        sl = slice(h * MEM_HEAD_DIM, (h + 1) * MEM_HEAD_DIM)
        s = _dot(qm[:, sl], mk_ref[sl, :])
        p = jnp.exp(s - jnp.max(s, axis=1, keepdims=True))
        p = p / jnp.sum(p, axis=1, keepdims=True)
        outs.append(_dot(p.astype(BF16), mv_ref[:, sl]))
    o = jnp.concatenate(outs, axis=1).astype(BF16)
    o_ref[...] = x1 + _dot(o, wo_ref[...])


def _mix_mem(x3, ylru, oc, osl, ow, gates, g_nsa, w_out, g_q, w_mq, mk, mv, w_mo):
    bsz, seq, _ = x3.shape
    tm = 256
    e = np.zeros((LANES, 3 * D_NSA), np.float32)
    for h in range(N_HEADS):
        for j in range(3):
            e[h * 3 + j, j * D_NSA + h * HEAD_DIM:j * D_NSA + (h + 1) * HEAD_DIM] = 1.0
    e = jnp.asarray(e, BF16)
    vec = lambda a: a.reshape(1, -1)
    wq = (w_mq * (MEM_HEAD_DIM ** -0.5)).astype(BF16)
    row = lambda n: pl.BlockSpec((None, tm, n), lambda b, i: (b, i, 0))
    full = lambda a: pl.BlockSpec(a.shape, lambda b, i: (0,) * a.ndim)
    memb = pl.BlockSpec((None, mv.shape[1], D_MODEL), lambda b, i: (b, 0, 0))
    membt = pl.BlockSpec((None, D_MODEL, mv.shape[1]), lambda b, i: (b, 0, 0))
    consts = [e, vec(g_nsa), w_out.astype(BF16), vec(g_q), wq]
    return pl.pallas_call(
        _mix_kernel,
        grid=(bsz, seq // tm),
        in_specs=[row(D_MODEL), row(D_LRU), row(D_NSA), row(D_NSA), row(D_NSA), row(LANES)]
        + [full(a) for a in consts] + [membt, memb, full(w_mo)],
        out_specs=row(D_MODEL),
        out_shape=jax.ShapeDtypeStruct(x3.shape, F32),
        compiler_params=_cparams(("parallel", "parallel")),
        name="mix_mem",
    )(x3, ylru, oc, osl, ow, gates.reshape(bsz, seq, LANES), *consts, mk, mv, w_mo)


def _peer_pairs():
    k = PEER_TOPK
    return [(a, b) for a in range(k) for b in range(k) if (a + 1) * (b + 1) <= k]


def _route_kernel(x_ref, g_ref, wpq_ref, sk_ref, xn_ref, r2_ref, e2_ref, ns_ref, sc_ref,
                  q_scr, s_scr, l_scr, t_scr, *, tt):
    k = PEER_TOPK
    nhc = 2 * PEER_HEADS
    xn32 = _rms(x_ref[...], g_ref[...])
    xn = xn32.astype(BF16)
    xn_ref[...] = xn32.T.astype(BF16)
    q = _dot(xn, wpq_ref[...]).astype(BF16)
    for hc in range(nhc):
        q_scr[hc] = q[:, hc * PEER_NKEYS:(hc + 1) * PEER_NKEYS]

    def score_and_sort(hc, carry):
        st = _dot_nt(sk_ref[hc], q_scr[hc])
        s_scr[hc] = st
        rows = _topk_rows(st, want_sorted=True)
        h, c = hc // 2, hc % 2
        for a in range(k):
            l_scr[c, a, pl.ds(h, 1), :] = rows[a][0:1, :]
        return carry

    lax.fori_loop(0, nhc, score_and_sort, 0)
    v1 = [l_scr[0, a] for a in range(k)]
    v2 = [l_scr[1, b] for b in range(k)]
    pairs = _peer_pairs()
    cand = [v1[a] + v2[b] for a, b in pairs]
    pad = [jnp.full_like(cand[0], -jnp.inf)] * (4 * k - len(cand))
    groups = [_sort_desc((cand + pad)[i * k:(i + 1) * k]) for i in range(4)]
    left = _bitonic_merge_desc(_top_half(groups[0], groups[1]))
    right = _bitonic_merge_desc(_top_half(groups[2], groups[3]))
    tau = functools.reduce(jnp.minimum, _top_half(left, right))
    top = v1[0] + v2[0]
    z = jnp.zeros_like(top)
    for c_ in cand:
        z = z + jnp.where(c_ >= tau, jnp.exp(c_ - top), 0.0)
    t_scr[0] = tau
    t_scr[1] = 1.0 / z

    def per_head(h, carry):
        s1, s2 = s_scr[2 * h], s_scr[2 * h + 1]
        tau_h = t_scr[0, pl.ds(h, 1), :]
        zinv_h = t_scr[1, pl.ds(h, 1), :]
        rank = jnp.zeros((PEER_NKEYS, tt), F32)
        cnt = jnp.zeros((PEER_NKEYS, tt), F32)
        for b in range(k):
            lb = l_scr[1, b, pl.ds(h, 1), :]
            rank = rank + (lb > s2).astype(F32)
            cnt = cnt + ((s1 + lb) >= tau_h).astype(F32)
        r2_ref[h] = rank.astype(r2_ref.dtype)
        ns_ref[h] = cnt
        e2_ref[h] = jnp.exp(s2 - l_scr[1, 0, pl.ds(h, 1), :]).astype(e2_ref.dtype)
        sc_ref[h] = jnp.exp(s1 - l_scr[0, 0, pl.ds(h, 1), :]) * zinv_h
        return carry

    lax.fori_loop(0, PEER_HEADS, per_head, 0)


def _peer_route(x2d, g_ffn, w_pq, sub_keys):
    m = x2d.shape[0]
    tt = min(512, m)
    g = g_ffn.reshape(1, D_MODEL)
    wpq = w_pq.astype(BF16)
    sk = sub_keys.astype(BF16).reshape(2 * PEER_HEADS, PEER_NKEYS, PEER_QDIM // 2)
    full = lambda a: pl.BlockSpec(a.shape, lambda i: (0,) * a.ndim)
    tcol = pl.BlockSpec((PEER_HEADS, PEER_NKEYS, tt), lambda i: (0, 0, i))
    shp = lambda dt: jax.ShapeDtypeStruct((PEER_HEADS, PEER_NKEYS, m), dt)
    return pl.pallas_call(
        functools.partial(_route_kernel, tt=tt),
        grid=(m // tt,),
        in_specs=[pl.BlockSpec((tt, D_MODEL), lambda i: (i, 0)), full(g), full(wpq), full(sk)],
        out_specs=[pl.BlockSpec((D_MODEL, tt), lambda i: (0, i)), tcol, tcol, tcol, tcol],
        out_shape=[jax.ShapeDtypeStruct((D_MODEL, m), BF16), shp(BF16), shp(BF16), shp(F32), shp(F32)],
        scratch_shapes=[pltpu.VMEM((2 * PEER_HEADS, tt, PEER_NKEYS), BF16),
                        pltpu.VMEM((2 * PEER_HEADS, PEER_NKEYS, tt), F32),
                        pltpu.VMEM((2, PEER_TOPK, PEER_HEADS, tt), F32),
                        pltpu.VMEM((2, PEER_HEADS, tt), F32)],
        compiler_params=_cparams(("parallel",)),
        name="peer_route",
    )(x2d, g, wpq, sk)


def _dense_kernel(x_ref, xn_ref, r2_ref, e2_ref, ns_ref, sc_ref, u_ref, vt_ref, gf_ref,
                  o_ref, acc_ref, a_ref, *, i1b, tt):
    ei = pl.program_id(1)

    @pl.when(ei == 0)
    def _():
        acc_ref[...] = jnp.zeros_like(acc_ref)

    gh = _gelu(_dot(u_ref[...], xn_ref[...]).astype(BF16))
    for kk in range(i1b):
        i1 = ei * i1b + kk
        w = None
        for h in range(PEER_HEADS):
            n_row = ns_ref[h, pl.ds(i1, 1), :].astype(BF16)
            s_row = sc_ref[h, pl.ds(i1, 1), :].astype(BF16)
            term = jnp.where(r2_ref[h] < n_row, e2_ref[h], jnp.zeros((), BF16)) * s_row
            w = term if w is None else w + term
        rs = slice(kk * PEER_NKEYS, (kk + 1) * PEER_NKEYS)
        a_ref[rs, :] = w * gh[rs, :]
    acc_ref[...] += _dot(vt_ref[...], a_ref[...])

    @pl.when(ei == pl.num_programs(1) - 1)
    def _():
        o_ref[...] = _rms(x_ref[...] + acc_ref[...].T, gf_ref[...])


def _peer_dense(x2d, xn, r2, e2, ns, sc, peer_u, peer_v, g_final):
    m = x2d.shape[0]
    tt = min(512, m)
    i1b = 8
    te = i1b * PEER_NKEYS
    u = peer_u.astype(BF16)
    vt = peer_v.T.astype(BF16)
    gf = g_final.reshape(1, D_MODEL)
    tok = pl.BlockSpec((tt, D_MODEL), lambda t, e: (t, 0))
    tcol = pl.BlockSpec((PEER_HEADS, PEER_NKEYS, tt), lambda t, e: (0, 0, t))
    return pl.pallas_call(
        functools.partial(_dense_kernel, i1b=i1b, tt=tt),
        grid=(m // tt, PEER_NKEYS // i1b),
        in_specs=[tok, pl.BlockSpec((D_MODEL, tt), lambda t, e: (0, t)), tcol, tcol, tcol, tcol,
                  pl.BlockSpec((te, D_MODEL), lambda t, e: (e, 0)),
                  pl.BlockSpec((D_MODEL, te), lambda t, e: (0, e)),
                  pl.BlockSpec((1, D_MODEL), lambda t, e: (0, 0))],
        out_specs=tok,
        out_shape=jax.ShapeDtypeStruct((m, D_MODEL), F32),
        scratch_shapes=[pltpu.VMEM((D_MODEL, tt), F32), pltpu.VMEM((te, tt), BF16)],
        compiler_params=_cparams(("parallel", "arbitrary")),
        name="peer_dense",
    )(x2d, xn, r2, e2, ns, sc, u, vt, gf)


def _per_kv_head(a, bsz, seq):
    return a.reshape(bsz, seq, N_KV, HEAD_DIM).transpose(0, 2, 1, 3)


def _layer(x, mem, p):
    bsz, seq, _ = x.shape
    m = bsz * seq
    xl, gl, q, kvc, ks, vs, kw, vw, gates = _in_proj(x.reshape(m, D_MODEL), p["g_mix"], p["w_in"],
                                                     p["b_gate"])
    ylru = _rg_lru(xl, gl, p["conv_w"], p["conv_b"], p["w_rg_a"], p["b_rg_a"], p["w_rg_i"],
                   p["b_rg_i"], p["lam"], p["g_out_lru"], bsz, seq)
    kc, vc = _compress(kvc, p["cmp_pos_k"], p["cmp_w1_k"], p["cmp_w2_k"], p["cmp_pos_v"],
                       p["cmp_w1_v"], p["cmp_w2_v"], bsz, seq)
    ncp = seq // CMP_STRIDE
    pad64 = lambda a: jnp.pad(a, ((0, 0),) * (a.ndim - 1) + ((0, LANES - HEAD_DIM),))
    swap = lambda a: jnp.swapaxes(a, -1, -2)
    kc4 = swap(pad64(kc.reshape(bsz, ncp, N_KV, HEAD_DIM).transpose(0, 2, 1, 3)))
    vc4 = vc.reshape(bsz, ncp, N_KV, HEAD_DIM).transpose(0, 2, 1, 3)
    q4 = pad64(q.reshape(bsz, seq, N_KV, GROUP, HEAD_DIM).transpose(0, 2, 3, 1, 4))
    o_cmp, bias = _nsa_select(q4, kc4, vc4, bsz, seq)

    nsb = LANES
    onehot = jnp.asarray(np.arange(seq)[:, None] // SEL_BLOCK == np.arange(nsb)[None, :], BF16)
    ones_col = jnp.asarray(np.arange(LANES - HEAD_DIM)[None, :] == 0, BF16)

    def with_ones(v):
        v4 = _per_kv_head(v, bsz, seq)
        return jnp.concatenate([v4, jnp.broadcast_to(ones_col, v4.shape[:3] + (LANES - HEAD_DIM,))], -1)

    ka = jnp.concatenate([pad64(_per_kv_head(ks, bsz, seq)),
                          jnp.broadcast_to(onehot, (bsz, N_KV, seq, nsb))], axis=-1)
    o_slc, o_win = _nsa_attend(q4, bias, swap(ka), with_ones(vs),
                               swap(pad64(_per_kv_head(kw, bsz, seq))), with_ones(vw), bsz, seq)

    mk, mv = _mem_kv(mem, p["g_mem_kv"], p["w_mk"].astype(BF16), p["w_mv"].astype(BF16))
    x2 = _mix_mem(x, ylru, o_cmp, o_slc, o_win, gates, p["g_out_nsa"], p["w_out"], p["g_mem_q"],
                  p["w_mq"], swap(mk), mv, p["w_mo"].astype(BF16))
    return x2.reshape(m, D_MODEL)


def kernel(x, mem, g_mix, w_in, b_gate, conv_w, conv_b, w_rg_a, b_rg_a, w_rg_i, b_rg_i, lam,
           cmp_pos_k, cmp_w1_k, cmp_w2_k, cmp_pos_v, cmp_w1_v, cmp_w2_v, g_out_lru, g_out_nsa,
           w_out, g_mem_q, g_mem_kv, w_mq, w_mk, w_mv, w_mo, g_ffn, w_pq, sub_keys, peer_u,
           peer_v, g_final):
    depth = w_in.shape[0]
    assert depth == 1, "the final norm is fused into the single layer's PEER kernel"
    names = ["g_mix", "w_in", "b_gate", "conv_w", "conv_b", "w_rg_a", "b_rg_a", "w_rg_i", "b_rg_i",
             "lam", "cmp_pos_k", "cmp_w1_k", "cmp_w2_k", "cmp_pos_v", "cmp_w1_v", "cmp_w2_v",
             "g_out_lru", "g_out_nsa", "w_out", "g_mem_q", "g_mem_kv", "w_mq", "w_mk", "w_mv", "w_mo",
             "g_ffn", "w_pq", "sub_keys", "peer_u", "peer_v"]
    vals = [g_mix, w_in, b_gate, conv_w, conv_b, w_rg_a, b_rg_a, w_rg_i, b_rg_i, lam, cmp_pos_k,
            cmp_w1_k, cmp_w2_k, cmp_pos_v, cmp_w1_v, cmp_w2_v, g_out_lru, g_out_nsa, w_out, g_mem_q,
            g_mem_kv, w_mq, w_mk, w_mv, w_mo, g_ffn, w_pq, sub_keys, peer_u, peer_v]
    p = {n: v[0] for n, v in zip(names, vals)}
    bsz, seq, _ = x.shape
    x2 = _layer(x, mem, p)
    xn, r2, e2, ns, sc = _peer_route(x2, p["g_ffn"], p["w_pq"], p["sub_keys"])
    out = _peer_dense(x2, xn, r2, e2, ns, sc, p["peer_u"], p["peer_v"], g_final)
    return out.reshape(bsz, seq, D_MODEL)
```

```python
import functools

import numpy as np
import jax
import jax.numpy as jnp
from jax import lax
from jax.experimental import pallas as pl
from jax.experimental.pallas import tpu as pltpu

F32 = jnp.float32
BF16 = jnp.bfloat16

D_MODEL = 1024
N_MEM = 256
D_LRU = 512
LRU_BLOCKS = 8
CONV_W = 4
LRU_C = 8.0
N_HEADS = 8
N_KV = 2
GROUP = N_HEADS // N_KV
HEAD_DIM = 64
D_NSA = N_HEADS * HEAD_DIM
CMP_BLOCK = 32
CMP_STRIDE = 16
SEL_BLOCK = 64
N_SELECT = 16
WINDOW = 512
MEM_HEADS = 4
MEM_HEAD_DIM = D_MODEL // MEM_HEADS
PEER_HEADS = 8
PEER_NKEYS = 128
PEER_TOPK = 16
PEER_QDIM = 256
EPS = 1e-6
KV_W = N_KV * HEAD_DIM
N_IN = 2 * D_LRU + D_NSA + 6 * KV_W + 3 * N_HEADS
N_IN_PAD = 2 * D_LRU + D_NSA + 6 * KV_W + 128

LANES = 128
SUBLANES = 8
NEG = -1e30
VMEM_LIMIT = 56 * 1024 * 1024

Q_TILE = 256
K_TILE = 512


def _cparams(sem):
    return pltpu.CompilerParams(dimension_semantics=sem, vmem_limit_bytes=VMEM_LIMIT)


def _rms(x, g):
    return x * lax.rsqrt(jnp.mean(x * x, axis=-1, keepdims=True) + EPS) * g


def _gelu(x):
    c = 0.7978845608028654
    return 0.5 * x * (1.0 + jnp.tanh(c * (x + 0.044715 * (x * x * x))))


def _dot(a, b):
    return jnp.dot(a, b, preferred_element_type=F32)


def _dot_nt(a, b):
    return lax.dot_general(a, b, (((1,), (1,)), ((), ())), preferred_element_type=F32)


def _split3(a):
    hi = a.astype(BF16)
    r1 = a - hi.astype(F32)
    mid = r1.astype(BF16)
    lo = (r1 - mid.astype(F32)).astype(BF16)
    return hi, mid, lo


def _batcher_pairs(n):
    pairs = []
    p = 1
    while p < n:
        k = p
        while k >= 1:
            for j in range(k % p, n - k, 2 * k):
                for i in range(min(k, n - j - k)):
                    if (i + j) // (2 * p) == (i + j + k) // (2 * p):
                        pairs.append((i + j, i + j + k))
            k //= 2
        p *= 2
    return pairs


_SORT16 = _batcher_pairs(16)


def _sort_desc(xs):
    xs = list(xs)
    for i, j in _batcher_pairs(len(xs)):
        a, b = xs[i], xs[j]
        xs[i], xs[j] = jnp.maximum(a, b), jnp.minimum(a, b)
    return xs


def _bitonic_merge_desc(xs):
    xs = list(xs)
    n = len(xs)
    stride = n // 2
    while stride >= 1:
        for i in range(n):
            if (i & stride) == 0:
                a, b = xs[i], xs[i + stride]
                xs[i], xs[i + stride] = jnp.maximum(a, b), jnp.minimum(a, b)
        stride //= 2
    return xs


def _top_half(a, b):
    n = len(a)
    return [jnp.maximum(a[i], b[n - 1 - i]) for i in range(n)]


def _topk_rows(x, *, want_sorted):
    k = PEER_TOPK
    rows = [x[SUBLANES * v:SUBLANES * (v + 1), :] for v in range(k)]
    rows = _sort_desc(rows)
    for step, shift in enumerate((4, 2, 1)):
        part = [pltpu.roll(r, shift, 0) for r in rows]
        rows = _top_half(rows, part)
        if step < 2 or want_sorted:
            rows = _bitonic_merge_desc(rows)
    if want_sorted:
        return rows
    return functools.reduce(jnp.minimum, rows)


def _inproj_kernel(x_ref, g_ref, w_ref, wkt_ref, bg_ref,
                   xl_ref, gl_ref, q_ref, kvc_ref, ka_ref, va_ref, kw_ref, vw_ref, gate_ref, *, tm):
    s0 = pl.program_id(1) * tm
    h = _rms(x_ref[...], g_ref[...]).astype(BF16)
    z = _dot(h, w_ref[...])
    kt = _dot_nt(wkt_ref[...], h)
    o = 0
    xl_ref[...] = z[:, o:o + D_LRU]; o += D_LRU
    gl_ref[...] = z[:, o:o + D_LRU]; o += D_LRU
    zq = z[:, o:o + D_NSA] * (HEAD_DIM ** -0.5); o += D_NSA
    kvc_ref[...] = z[:, o:o + 2 * KV_W]; o += 2 * KV_W
    zvs = z[:, o:o + KV_W]; o += KV_W
    zvw = z[:, o:o + KV_W]; o += KV_W
    gate_ref[...] = jax.nn.sigmoid(z[:, o:o + LANES] + bg_ref[...])

    lane = lax.broadcasted_iota(jnp.int32, (tm, LANES), 1)
    low = lane < HEAD_DIM
    ones_col = jnp.where(lane == HEAD_DIM, 1.0, 0.0)

    def head_block(src, idx):
        blk = src[:, (idx // 2) * LANES:(idx // 2 + 1) * LANES]
        return pltpu.roll(blk, HEAD_DIM, 1) if idx % 2 else blk

    for hh in range(N_HEADS):
        q_ref[hh // GROUP, hh % GROUP] = jnp.where(low, head_block(zq, hh), 0.0).astype(BF16)
    blk_id = lax.broadcasted_iota(jnp.int32, (LANES, tm), 0)
    tok = s0 + lax.broadcasted_iota(jnp.int32, (LANES, tm), 1)
    onehot = jnp.where(tok // SEL_BLOCK == blk_id, 1.0, 0.0).astype(BF16)
    zeros = jnp.zeros((HEAD_DIM, tm), BF16)
    for kv in range(N_KV):
        va_ref[kv] = jnp.where(low, head_block(zvs, kv), ones_col).astype(BF16)
        vw_ref[kv] = jnp.where(low, head_block(zvw, kv), ones_col).astype(BF16)
        ka_ref[kv, 0:HEAD_DIM, :] = kt[kv * HEAD_DIM:(kv + 1) * HEAD_DIM, :].astype(BF16)
        ka_ref[kv, HEAD_DIM:LANES, :] = zeros
        ka_ref[kv, LANES:2 * LANES, :] = onehot
        kw_ref[kv, 0:HEAD_DIM, :] = kt[KV_W + kv * HEAD_DIM:KV_W + (kv + 1) * HEAD_DIM, :].astype(BF16)
        kw_ref[kv, HEAD_DIM:LANES, :] = zeros


def _in_proj(x, g_mix, w_in, b_gate):
    bsz, seq, _ = x.shape
    tm = 512
    o_k = 2 * D_LRU + D_NSA + 2 * KV_W
    cols = lambda a, n: w_in[:, a:a + n]
    w = jnp.concatenate([cols(0, o_k), cols(o_k + KV_W, KV_W), cols(o_k + 3 * KV_W, KV_W),
                         cols(o_k + 4 * KV_W, 3 * N_HEADS)], axis=1)
    w = jnp.pad(w, ((0, 0), (0, LANES - 3 * N_HEADS))).astype(BF16)
    wkt = jnp.concatenate([cols(o_k, KV_W), cols(o_k + 2 * KV_W, KV_W)], axis=1).T.astype(BF16)
    bg = jnp.pad(b_gate, (0, LANES - 3 * N_HEADS)).reshape(1, LANES)
    row = lambda n: pl.BlockSpec((None, tm, n), lambda b, i: (b, i, 0))
    full = lambda a: pl.BlockSpec(a.shape, lambda b, i: (0,) * a.ndim)
    g = g_mix.reshape(1, D_MODEL)
    sds = jax.ShapeDtypeStruct
    out_shape = [sds((bsz, seq, D_LRU), F32), sds((bsz, seq, D_LRU), F32),
                 sds((bsz, N_KV, GROUP, seq, LANES), BF16), sds((bsz, seq, 2 * KV_W), F32),
                 sds((bsz, N_KV, 2 * LANES, seq), BF16), sds((bsz, N_KV, seq, LANES), BF16),
                 sds((bsz, N_KV, LANES, seq), BF16), sds((bsz, N_KV, seq, LANES), BF16),
                 sds((bsz, seq, LANES), F32)]
    keys_t = lambda n: pl.BlockSpec((None, N_KV, n, tm), lambda b, i: (b, 0, 0, i))
    vals = pl.BlockSpec((None, N_KV, tm, LANES), lambda b, i: (b, 0, i, 0))
    out_specs = [row(D_LRU), row(D_LRU),
                 pl.BlockSpec((None, N_KV, GROUP, tm, LANES), lambda b, i: (b, 0, 0, i, 0)),
                 row(2 * KV_W), keys_t(2 * LANES), vals, keys_t(LANES), vals, row(LANES)]
    return pl.pallas_call(
        functools.partial(_inproj_kernel, tm=tm),
        grid=(bsz, seq // tm),
        in_specs=[row(D_MODEL), full(g), full(w), full(wkt), full(bg)],
        out_specs=out_specs,
        out_shape=out_shape,
        compiler_params=_cparams(("parallel", "parallel")),
        name="in_proj",
    )(x, g, w, wkt, bg)


def _rglru_kernel(x_ref, xprev_ref, gate_ref, cw_ref, cb_ref, wa_ref, ba_ref, wi_ref, bi_ref,
                  lam_ref, gout_ref, o_ref, h_ref, *, t):
    i = pl.program_id(1)

    @pl.when(i == 0)
    def _():
        h_ref[...] = jnp.zeros_like(h_ref)

    x = x_ref[...]
    prev = jnp.where(i > 0, xprev_ref[...], 0.0)
    xs = jnp.concatenate([prev, x], axis=0)
    cw = cw_ref[...]
    xc = cb_ref[...] + x * cw[CONV_W - 1:CONV_W, :]
    for k in range(1, CONV_W):
        xk = pltpu.roll(xs, k, 0)[SUBLANES:, :]
        xc = xc + xk * cw[CONV_W - 1 - k:CONV_W - k, :]
    xcb = xc.astype(BF16)
    r = jax.nn.sigmoid(_dot(xcb, wa_ref[...]) + ba_ref[...])
    gi = jax.nn.sigmoid(_dot(xcb, wi_ref[...]) + bi_ref[...])
    lam = lam_ref[...]
    softplus = jnp.maximum(-lam, 0.0) + jnp.log1p(jnp.exp(-jnp.abs(lam)))
    log_a = (-LRU_C) * r * softplus
    a = jnp.exp(log_a)
    b = jnp.sqrt(-jnp.tanh(log_a) * (a * a + 1.0)) * (gi * xc)
    rowi = lax.broadcasted_iota(jnp.int32, (t, D_LRU), 0)
    d = 1
    while d < t:
        ok = rowi >= d
        a_s = pltpu.roll(a, d, 0)
        b_s = pltpu.roll(b, d, 0)
        b = jnp.where(ok, a * b_s, 0.0) + b
        a = jnp.where(ok, a * a_s, a)
        d *= 2
    h = b + a * h_ref[0:1, :]
    h_ref[...] = jnp.broadcast_to(h[t - 1:t, :], h_ref.shape)
    y = h * _gelu(gate_ref[...])
    o_ref[...] = _rms(y, gout_ref[...]).astype(o_ref.dtype)


def _block_diag(w):
    nb, bd, _ = w.shape
    eye = jnp.eye(nb, dtype=w.dtype)
    return (eye[:, None, :, None] * w[:, :, None, :]).reshape(nb * bd, nb * bd)


def _rg_lru(xl, gl, conv_w, conv_b, w_a, b_a, w_i, b_i, lam, g_out, bsz, seq):
    t = 256
    xl3 = xl.reshape(bsz, seq, D_LRU)
    gl3 = gl.reshape(bsz, seq, D_LRU)
    cw = jnp.pad(conv_w, ((0, SUBLANES - CONV_W), (0, 0)))
    vec = lambda a: a.reshape(1, D_LRU)
    wa = _block_diag(w_a).astype(BF16)
    wi = _block_diag(w_i).astype(BF16)
    args = [cw, vec(conv_b), wa, vec(b_a), wi, vec(b_i), vec(lam), vec(g_out)]
    full = lambda a: pl.BlockSpec(a.shape, lambda b, i: (0,) * a.ndim)
    tile = pl.BlockSpec((None, t, D_LRU), lambda b, i: (b, i, 0))
    prev = pl.BlockSpec((None, SUBLANES, D_LRU),
                        lambda b, i: (b, jnp.maximum(i * (t // SUBLANES) - 1, 0), 0))
    return pl.pallas_call(
        functools.partial(_rglru_kernel, t=t),
        grid=(bsz, seq // t),
        in_specs=[tile, prev, tile] + [full(a) for a in args],
        out_specs=tile,
        out_shape=jax.ShapeDtypeStruct((bsz, seq, D_LRU), BF16),
        scratch_shapes=[pltpu.VMEM((SUBLANES, D_LRU), F32)],
        compiler_params=_cparams(("parallel", "arbitrary")),
        name="rg_lru",
    )(xl3, xl3, gl3, *args)


def _compress_kernel(r_ref, pa_ref, pb_ref, w1a_ref, w1b_ref, w2_ref, k_ref, v_ref):
    r = r_ref[...]
    n = r.shape[0]

    def dot3(x, w_ref):
        out = None
        for piece in _split3(x):
            for wi in range(3):
                term = _dot(piece, w_ref[wi])
                out = term if out is None else out + term
        return out

    first = dot3(r + pa_ref[...], w1a_ref)
    second = dot3(r + pb_ref[...], w1b_ref)
    pre = first + pltpu.roll(second, n - 1, 0)
    out = dot3(_gelu(pre), w2_ref)
    k_ref[...] = out[:, :KV_W]
    v_ref[...] = out[:, KV_W:].astype(v_ref.dtype)


def _compress(kvc, pos_k, w1_k, w2_k, pos_v, w1_v, w2_v, bsz, seq):
    nrow = seq // CMP_STRIDE
    width = CMP_STRIDE * 2 * KV_W
    r = kvc.reshape(bsz, nrow, width)
    eye = jnp.eye(N_KV, dtype=F32)

    def big1(w1, half):
        w = w1.reshape(CMP_BLOCK, HEAD_DIM, HEAD_DIM)[half * CMP_STRIDE:(half + 1) * CMP_STRIDE]
        return jnp.einsum("ldo,hg->lhdgo", w, eye).reshape(CMP_STRIDE, KV_W, KV_W)

    def both1(half):
        z = jnp.zeros((CMP_STRIDE, KV_W, KV_W), F32)
        top = jnp.concatenate([big1(w1_k, half), z], axis=2)
        bot = jnp.concatenate([z, big1(w1_v, half)], axis=2)
        return jnp.concatenate([top, bot], axis=1).reshape(width, 2 * KV_W)

    def pos_row(half):
        pk = jnp.tile(pos_k[half * CMP_STRIDE:(half + 1) * CMP_STRIDE, None, :], (1, N_KV, 1))
        pv = jnp.tile(pos_v[half * CMP_STRIDE:(half + 1) * CMP_STRIDE, None, :], (1, N_KV, 1))
        return jnp.concatenate([pk, pv], axis=1).reshape(1, width)

    def w2big():
        zk = jnp.zeros((KV_W, KV_W), F32)
        k2 = jnp.einsum("do,hg->hdgo", w2_k, eye).reshape(KV_W, KV_W)
        v2 = jnp.einsum("do,hg->hdgo", w2_v, eye).reshape(KV_W, KV_W)
        return jnp.concatenate([jnp.concatenate([k2, zk], 1), jnp.concatenate([zk, v2], 1)], 0)

    pieces = lambda w: jnp.stack(_split3(w))
    args = [pos_row(0), pos_row(1), pieces(both1(0)), pieces(both1(1)), pieces(w2big())]
    full = lambda a: pl.BlockSpec(a.shape, lambda b: (0,) * a.ndim)
    return pl.pallas_call(
        _compress_kernel,
        grid=(bsz,),
        in_specs=[pl.BlockSpec((None, nrow, width), lambda b: (b, 0, 0))] + [full(a) for a in args],
        out_specs=[pl.BlockSpec((None, nrow, KV_W), lambda b: (b, 0, 0))] * 2,
        out_shape=[jax.ShapeDtypeStruct((bsz, nrow, KV_W), F32),
                   jax.ShapeDtypeStruct((bsz, nrow, KV_W), BF16)],
        compiler_params=_cparams(("parallel",)),
        name="compress",
    )(r, *args)


def _heads_to_lanes(o):
    return jnp.concatenate([o[g * Q_TILE:(g + 1) * Q_TILE, :] for g in range(GROUP)], axis=1)


def _select_kernel(q_ref, kc_ref, vc_ref, ovl_ref, tri_ref, o_ref, bias_ref, *, nsb):
    s0 = pl.program_id(2) * Q_TILE
    rows = GROUP * Q_TILE
    ncp = kc_ref.shape[1]
    q = q_ref[...].reshape(rows, LANES)
    kc_hi, kc_mid, kc_lo = _split3(kc_ref[...])
    s = _dot(q, kc_hi) + _dot(q, kc_mid) + _dot(q, kc_lo)
    t = s0 + (lax.broadcasted_iota(jnp.int32, (rows, ncp), 0) & (Q_TILE - 1))
    cend = lax.broadcasted_iota(jnp.int32, (rows, ncp), 1) * CMP_STRIDE + (CMP_BLOCK - 1)
    mask = cend <= t
    sm = jnp.where(mask, s, NEG)
    mx = jnp.max(sm, axis=1, keepdims=True)
    p = jnp.where(mask, jnp.exp(sm - mx), 0.0)
    l = jnp.sum(p, axis=1, keepdims=True)
    pc = p / jnp.maximum(l, 1e-30)
    o = _dot(pc.astype(BF16), vc_ref[...])
    o_ref[...] = _heads_to_lanes(o)
    psum = pc[0:Q_TILE]
    for g in range(1, GROUP):
        psum = psum + pc[g * Q_TILE:(g + 1) * Q_TILE]
    ovl = ovl_ref[...]
    imp = None
    for piece in _split3(psum):
        term = _dot(piece, ovl)
        imp = term if imp is None else imp + term
    tq = s0 + lax.broadcasted_iota(jnp.int32, (Q_TILE, nsb), 0)
    jb = lax.broadcasted_iota(jnp.int32, (Q_TILE, nsb), 1)
    cur = tq // SEL_BLOCK
    forced = (jb == 0) | (jb == cur) | (jb == cur - 1)
    val = jnp.where(jb <= cur, jnp.where(forced, jnp.inf, imp), -jnp.inf)
    vt = val.T
    tau = _topk_rows(vt, want_sorted=False)[0:1, :]
    gt = vt > tau
    tie = vt == tau
    need = float(N_SELECT) - jnp.sum(gt.astype(F32), axis=0, keepdims=True)
    pre = _dot(tri_ref[...], tie.astype(BF16))
    sel = (gt | (tie & (pre <= need))) & (vt > -1.0)
    bias_ref[...] = jnp.where(sel, 0.0, NEG).T.astype(bias_ref.dtype)


def _nsa_select(q4, kc4, vc4, bsz, seq):
    assert seq // SEL_BLOCK <= LANES, "block-selection kernel holds at most 128 selection blocks"
    nsb = LANES
    ncp = kc4.shape[3]
    ci = np.arange(ncp)[:, None] * CMP_STRIDE
    bj = np.arange(nsb)[None, :] * SEL_BLOCK
    ovl = ((ci < bj + SEL_BLOCK) & (ci + CMP_BLOCK > bj) & (np.arange(ncp)[:, None] < ncp - 1))
    ovl = jnp.asarray(ovl, BF16)
    tri = jnp.asarray(np.tril(np.ones((nsb, nsb))), BF16)
    full = lambda a: pl.BlockSpec(a.shape, lambda b, k, i: (0,) * a.ndim)
    return pl.pallas_call(
        functools.partial(_select_kernel, nsb=nsb),
        grid=(bsz, N_KV, seq // Q_TILE),
        in_specs=[pl.BlockSpec((None, None, GROUP, Q_TILE, LANES), lambda b, k, i: (b, k, 0, i, 0)),
                  pl.BlockSpec((None, None, LANES, ncp), lambda b, k, i: (b, k, 0, 0)),
                  pl.BlockSpec((None, None, ncp, HEAD_DIM), lambda b, k, i: (b, k, 0, 0)),
                  full(ovl), full(tri)],
        out_specs=[pl.BlockSpec((None, Q_TILE, GROUP * HEAD_DIM), lambda b, k, i: (b, i, k)),
                   pl.BlockSpec((None, None, Q_TILE, nsb), lambda b, k, i: (b, k, i, 0))],
        out_shape=[jax.ShapeDtypeStruct((bsz, seq, D_NSA), F32),
                   jax.ShapeDtypeStruct((bsz, N_KV, seq, nsb), BF16)],
        compiler_params=_cparams(("parallel", "parallel", "parallel")),
        name="nsa_select",
    )(q4, kc4, vc4, ovl, tri)


def _attend_kernel(q_ref, bias_ref, ka_ref, va_ref, kw_ref, vw_ref, os_ref, ow_ref):
    s0 = pl.program_id(2) * Q_TILE
    rows = GROUP * Q_TILE
    q = q_ref[...].reshape(rows, LANES)
    bias = bias_ref[...]
    qa = jnp.concatenate([q, jnp.concatenate([bias] * GROUP, axis=0)], axis=1)
    t_row = s0 + (lax.broadcasted_iota(jnp.int32, (rows, 1), 0) & (Q_TILE - 1))

    def step(kt, carry, masked):
        m, acc = carry
        k0 = pl.multiple_of(kt * K_TILE, K_TILE)
        s = _dot(qa, ka_ref[:, pl.ds(k0, K_TILE)])
        if masked:
            kpos = k0 + lax.broadcasted_iota(jnp.int32, (rows, K_TILE), 1)
            s = jnp.where(kpos <= t_row, s, NEG)
        mn = jnp.maximum(m, jnp.max(s, axis=1, keepdims=True))
        p = jnp.exp(s - mn)
        acc = jnp.exp(m - mn) * acc + _dot(p.astype(BF16), va_ref[pl.ds(k0, K_TILE), :])
        return mn, acc

    n_full = s0 // K_TILE
    init = (jnp.full((rows, 1), NEG, F32), jnp.zeros((rows, LANES), F32))
    carry = lax.fori_loop(0, n_full, lambda kt, c: step(kt, c, False), init)
    _, acc = step(n_full, carry, True)
    os_ref[...] = _heads_to_lanes(acc[:, :HEAD_DIM] / acc[:, HEAD_DIM:HEAD_DIM + 1])

    span = WINDOW + Q_TILE
    w0 = pl.multiple_of(jnp.maximum(s0 - WINDOW, 0), Q_TILE)
    s = _dot(q, kw_ref[:, pl.ds(w0, span)])
    kpos = w0 + lax.broadcasted_iota(jnp.int32, (rows, span), 1)
    mask = (kpos <= t_row) & (kpos > t_row - WINDOW)
    s = jnp.where(mask, s, NEG)
    p = jnp.where(mask, jnp.exp(s - jnp.max(s, axis=1, keepdims=True)), 0.0)
    acc = _dot(p.astype(BF16), vw_ref[pl.ds(w0, span), :])
    ow_ref[...] = _heads_to_lanes(acc[:, :HEAD_DIM] / acc[:, HEAD_DIM:HEAD_DIM + 1])


def _nsa_attend(q4, bias, ka, va, kw, vw, bsz, seq):
    per_head = lambda n: pl.BlockSpec((None, None, seq, n), lambda b, k, i: (b, k, 0, 0))
    per_head_t = lambda n: pl.BlockSpec((None, None, n, seq), lambda b, k, i: (b, k, 0, 0))
    out = pl.BlockSpec((None, Q_TILE, GROUP * HEAD_DIM), lambda b, k, i: (b, i, k))
    return pl.pallas_call(
        _attend_kernel,
        grid=(bsz, N_KV, seq // Q_TILE),
        in_specs=[pl.BlockSpec((None, None, GROUP, Q_TILE, LANES), lambda b, k, i: (b, k, 0, i, 0)),
                  pl.BlockSpec((None, None, Q_TILE, LANES), lambda b, k, i: (b, k, i, 0)),
                  per_head_t(2 * LANES), per_head(LANES), per_head_t(LANES), per_head(LANES)],
        out_specs=[out, out],
        out_shape=[jax.ShapeDtypeStruct((bsz, seq, D_NSA), F32)] * 2,
        compiler_params=_cparams(("parallel", "parallel", "arbitrary")),
        name="nsa_attend",
    )(q4, bias, ka, va, kw, vw)


def _memkv_kernel(m_ref, g_ref, wk_ref, wv_ref, k_ref, v_ref):
    mn = _rms(m_ref[...], g_ref[...]).astype(BF16)
    k_ref[...] = _dot(mn, wk_ref[...]).astype(k_ref.dtype)
    v_ref[...] = _dot(mn, wv_ref[...]).astype(v_ref.dtype)


def _mem_kv(mem, g, wk, wv):
    bsz, nm, _ = mem.shape
    g2 = g.reshape(1, D_MODEL)
    full = lambda a: pl.BlockSpec(a.shape, lambda b: (0,) * a.ndim)
    blk = pl.BlockSpec((None, nm, D_MODEL), lambda b: (b, 0, 0))
    return pl.pallas_call(
        _memkv_kernel,
        grid=(bsz,),
        in_specs=[blk, full(g2), full(wk), full(wv)],
        out_specs=[blk, blk],
        out_shape=[jax.ShapeDtypeStruct(mem.shape, BF16)] * 2,
        compiler_params=_cparams(("parallel",)),
        name="mem_kv",
    )(mem, g2, wk, wv)


def _mix_kernel(x_ref, ylru_ref, oc_ref, os_ref, ow_ref, gate_ref, e_ref, gnsa_ref, wout_ref,
                gq_ref, wq_ref, mk_ref, mv_ref, wo_ref, o_ref):
    g = gate_ref[...]
    g_hi = g.astype(BF16)
    g_lo = (g - g_hi.astype(F32)).astype(BF16)
    e = e_ref[...]
    gx = _dot(g_hi, e) + _dot(g_lo, e)
    y = (gx[:, 0:D_NSA] * oc_ref[...] + gx[:, D_NSA:2 * D_NSA] * os_ref[...]
         + gx[:, 2 * D_NSA:3 * D_NSA] * ow_ref[...])
    yn = _rms(y, gnsa_ref[...]).astype(BF16)
    ycat = jnp.concatenate([ylru_ref[...], yn], axis=1)
    x1 = x_ref[...] + _dot(ycat, wout_ref[...])
    xq = _rms(x1, gq_ref[...]).astype(BF16)
    qm = _dot(xq, wq_ref[...]).astype(BF16)
    outs = []
    for h in range(MEM_HEADS):
        sl = slice(h * MEM_HEAD_DIM, (h + 1) * MEM_HEAD_DIM)
        s = _dot(qm[:, sl], mk_ref[sl, :])
        p = jnp.exp(s - jnp.max(s, axis=1, keepdims=True))
        p = p / jnp.sum(p, axis=1, keepdims=True)
        outs.append(_dot(p.astype(BF16), mv_ref[:, sl]))
    o = jnp.concatenate(outs, axis=1).astype(BF16)
    o_ref[...] = x1 + _dot(o, wo_ref[...])


def _mix_mem(x3, ylru, oc, osl, ow, gates, g_nsa, w_out, g_q, w_mq, mk, mv, w_mo):
    bsz, seq, _ = x3.shape
    tm = 256
    e = np.zeros((LANES, 3 * D_NSA), np.float32)
    for h in range(N_HEADS):
        for j in range(3):
            e[h * 3 + j, j * D_NSA + h * HEAD_DIM:j * D_NSA + (h + 1) * HEAD_DIM] = 1.0
    e = jnp.asarray(e, BF16)
    vec = lambda a: a.reshape(1, -1)
    wq = (w_mq * (MEM_HEAD_DIM ** -0.5)).astype(BF16)
    row = lambda n: pl.BlockSpec((None, tm, n), lambda b, i: (b, i, 0))
    full = lambda a: pl.BlockSpec(a.shape, lambda b, i: (0,) * a.ndim)
    memb = pl.BlockSpec((None, mv.shape[1], D_MODEL), lambda b, i: (b, 0, 0))
    membt = pl.BlockSpec((None, D_MODEL, mv.shape[1]), lambda b, i: (b, 0, 0))
    consts = [e, vec(g_nsa), w_out.astype(BF16), vec(g_q), wq]
    return pl.pallas_call(
        _mix_kernel,
        grid=(bsz, seq // tm),
        in_specs=[row(D_MODEL), row(D_LRU), row(D_NSA), row(D_NSA), row(D_NSA), row(LANES)]
        + [full(a) for a in consts] + [membt, memb, full(w_mo)],
        out_specs=row(D_MODEL),
        out_shape=jax.ShapeDtypeStruct(x3.shape, F32),
        compiler_params=_cparams(("parallel", "parallel")),
        name="mix_mem",
    )(x3, ylru, oc, osl, ow, gates.reshape(bsz, seq, LANES), *consts, mk, mv, w_mo)


def _prefix_count(rows, test):
    pick = jnp.where
    b3 = test(rows[7])
    b2 = test(pick(b3, rows[11], rows[3]))
    b1 = test(pick(b3, pick(b2, rows[13], rows[9]), pick(b2, rows[5], rows[1])))
    b0 = test(pick(b3, pick(b2, pick(b1, rows[14], rows[12]), pick(b1, rows[10], rows[8])),
                   pick(b2, pick(b1, rows[6], rows[4]), pick(b1, rows[2], rows[0]))))
    cnt = pick(b3, 8.0, 0.0) + pick(b2, 4.0, 0.0) + pick(b1, 2.0, 0.0) + pick(b0, 1.0, 0.0)
    return pick(test(rows[15]), 16.0, cnt)


def _peer_pairs():
    k = PEER_TOPK
    return [(a, b) for a in range(k) for b in range(k) if (a + 1) * (b + 1) <= k]


def _route_kernel(x_ref, g_ref, wpq_ref, sk_ref, xn_ref, r2_ref, e2_ref, ns_ref, sc_ref,
                  q_scr, s_scr, l_scr, t_scr, *, tt):
    k = PEER_TOPK
    nhc = 2 * PEER_HEADS
    xn32 = _rms(x_ref[...], g_ref[...])
    xn = xn32.astype(BF16)
    xn_ref[...] = xn32.T.astype(BF16)
    q = _dot(xn, wpq_ref[...]).astype(BF16)
    for hc in range(nhc):
        q_scr[hc] = q[:, hc * PEER_NKEYS:(hc + 1) * PEER_NKEYS]

    def score_and_sort(hc, carry):
        st = _dot_nt(sk_ref[hc], q_scr[hc])
        s_scr[hc] = st
        rows = _topk_rows(st, want_sorted=True)
        h, c = hc // 2, hc % 2
        for a in range(k):
            l_scr[c, a, pl.ds(h, 1), :] = rows[a][0:1, :]
        return carry

    lax.fori_loop(0, nhc, score_and_sort, 0)
    v1 = [l_scr[0, a] for a in range(k)]
    v2 = [l_scr[1, b] for b in range(k)]
    pairs = _peer_pairs()
    cand = [v1[a] + v2[b] for a, b in pairs]
    pad = [jnp.full_like(cand[0], -jnp.inf)] * (4 * k - len(cand))
    groups = [_sort_desc((cand + pad)[i * k:(i + 1) * k]) for i in range(4)]
    left = _bitonic_merge_desc(_top_half(groups[0], groups[1]))
    right = _bitonic_merge_desc(_top_half(groups[2], groups[3]))
    tau = functools.reduce(jnp.minimum, _top_half(left, right))
    top = v1[0] + v2[0]
    z = jnp.zeros_like(top)
    for c_ in cand:
        z = z + jnp.where(c_ >= tau, jnp.exp(c_ - top), 0.0)
    t_scr[0] = tau
    t_scr[1] = 1.0 / z

    def per_head(h, carry):
        s1, s2 = s_scr[2 * h], s_scr[2 * h + 1]
        tau_h = t_scr[0, pl.ds(h, 1), :]
        zinv_h = t_scr[1, pl.ds(h, 1), :]
        l2 = [l_scr[1, b, pl.ds(h, 1), :] for b in range(k)]
        r2_ref[h] = _prefix_count(l2, lambda v: v > s2).astype(r2_ref.dtype)
        ns_ref[h] = _prefix_count(l2, lambda v: (s1 + v) >= tau_h)
        e2_ref[h] = jnp.exp(s2 - l_scr[1, 0, pl.ds(h, 1), :]).astype(e2_ref.dtype)
        sc_ref[h] = jnp.exp(s1 - l_scr[0, 0, pl.ds(h, 1), :]) * zinv_h
        return carry

    lax.fori_loop(0, PEER_HEADS, per_head, 0)


def _peer_route(x2d, g_ffn, w_pq, sub_keys):
    m = x2d.shape[0]
    tt = min(512, m)
    g = g_ffn.reshape(1, D_MODEL)
    wpq = w_pq.astype(BF16)
    sk = sub_keys.astype(BF16).reshape(2 * PEER_HEADS, PEER_NKEYS, PEER_QDIM // 2)
    full = lambda a: pl.BlockSpec(a.shape, lambda i: (0,) * a.ndim)
    tcol = pl.BlockSpec((PEER_HEADS, PEER_NKEYS, tt), lambda i: (0, 0, i))
    shp = lambda dt: jax.ShapeDtypeStruct((PEER_HEADS, PEER_NKEYS, m), dt)
    return pl.pallas_call(
        functools.partial(_route_kernel, tt=tt),
        grid=(m // tt,),
        in_specs=[pl.BlockSpec((tt, D_MODEL), lambda i: (i, 0)), full(g), full(wpq), full(sk)],
        out_specs=[pl.BlockSpec((D_MODEL, tt), lambda i: (0, i)), tcol, tcol, tcol, tcol],
        out_shape=[jax.ShapeDtypeStruct((D_MODEL, m), BF16), shp(BF16), shp(BF16), shp(F32), shp(F32)],
        scratch_shapes=[pltpu.VMEM((2 * PEER_HEADS, tt, PEER_NKEYS), BF16),
                        pltpu.VMEM((2 * PEER_HEADS, PEER_NKEYS, tt), F32),
                        pltpu.VMEM((2, PEER_TOPK, PEER_HEADS, tt), F32),
                        pltpu.VMEM((2, PEER_HEADS, tt), F32)],
        compiler_params=_cparams(("parallel",)),
        name="peer_route",
    )(x2d, g, wpq, sk)


def _dense_kernel(x_ref, xn_ref, r2_ref, e2_ref, ns_ref, sc_ref, u_ref, vt_ref, gf_ref,
                  o_ref, acc_ref, a_ref, *, i1b, tt):
    ei = pl.program_id(1)

    @pl.when(ei == 0)
    def _():
        acc_ref[...] = jnp.zeros_like(acc_ref)

    gh = _gelu(_dot(u_ref[...], xn_ref[...]).astype(BF16))
    for kk in range(i1b):
        i1 = ei * i1b + kk
        w = None
        for h in range(PEER_HEADS):
            n_row = ns_ref[h, pl.ds(i1, 1), :].astype(BF16)
            s_row = sc_ref[h, pl.ds(i1, 1), :].astype(BF16)
            term = jnp.where(r2_ref[h] < n_row, e2_ref[h], jnp.zeros((), BF16)) * s_row
            w = term if w is None else w + term
        rs = slice(kk * PEER_NKEYS, (kk + 1) * PEER_NKEYS)
        a_ref[rs, :] = w * gh[rs, :]
    acc_ref[...] += _dot(vt_ref[...], a_ref[...])

    @pl.when(ei == pl.num_programs(1) - 1)
    def _():
        o_ref[...] = _rms(x_ref[...] + acc_ref[...].T, gf_ref[...])


def _peer_dense(x2d, xn, r2, e2, ns, sc, peer_u, peer_v, g_final):
    m = x2d.shape[0]
    tt = min(512, m)
    i1b = 8
    te = i1b * PEER_NKEYS
    u = peer_u.astype(BF16)
    vt = peer_v.T.astype(BF16)
    gf = g_final.reshape(1, D_MODEL)
    tok = pl.BlockSpec((tt, D_MODEL), lambda t, e: (t, 0))
    tcol = pl.BlockSpec((PEER_HEADS, PEER_NKEYS, tt), lambda t, e: (0, 0, t))
    return pl.pallas_call(
        functools.partial(_dense_kernel, i1b=i1b, tt=tt),
        grid=(m // tt, PEER_NKEYS // i1b),
        in_specs=[tok, pl.BlockSpec((D_MODEL, tt), lambda t, e: (0, t)), tcol, tcol, tcol, tcol,
                  pl.BlockSpec((te, D_MODEL), lambda t, e: (e, 0)),
                  pl.BlockSpec((D_MODEL, te), lambda t, e: (0, e)),
                  pl.BlockSpec((1, D_MODEL), lambda t, e: (0, 0))],
        out_specs=tok,
        out_shape=jax.ShapeDtypeStruct((m, D_MODEL), F32),
        scratch_shapes=[pltpu.VMEM((D_MODEL, tt), F32), pltpu.VMEM((te, tt), BF16)],
        compiler_params=_cparams(("parallel", "arbitrary")),
        name="peer_dense",
    )(x2d, xn, r2, e2, ns, sc, u, vt, gf)


def _per_kv_head(a, bsz, seq):
    return a.reshape(bsz, seq, N_KV, HEAD_DIM).transpose(0, 2, 1, 3)


def _layer(x, mem, p):
    bsz, seq, _ = x.shape
    m = bsz * seq
    xl, gl, q4, kvc, ka, va, kw, vw, gates = _in_proj(x, p["g_mix"], p["w_in"], p["b_gate"])
    ylru = _rg_lru(xl, gl, p["conv_w"], p["conv_b"], p["w_rg_a"], p["b_rg_a"], p["w_rg_i"],
                   p["b_rg_i"], p["lam"], p["g_out_lru"], bsz, seq)
    kc, vc = _compress(kvc, p["cmp_pos_k"], p["cmp_w1_k"], p["cmp_w2_k"], p["cmp_pos_v"],
                       p["cmp_w1_v"], p["cmp_w2_v"], bsz, seq)
    ncp = seq // CMP_STRIDE
    pad64 = lambda a: jnp.pad(a, ((0, 0),) * (a.ndim - 1) + ((0, LANES - HEAD_DIM),))
    swap = lambda a: jnp.swapaxes(a, -1, -2)
    kc4 = swap(pad64(kc.reshape(bsz, ncp, N_KV, HEAD_DIM).transpose(0, 2, 1, 3)))
    vc4 = vc.reshape(bsz, ncp, N_KV, HEAD_DIM).transpose(0, 2, 1, 3)
    o_cmp, bias = _nsa_select(q4, kc4, vc4, bsz, seq)
    o_slc, o_win = _nsa_attend(q4, bias, ka, va, kw, vw, bsz, seq)

    mk, mv = _mem_kv(mem, p["g_mem_kv"], p["w_mk"].astype(BF16), p["w_mv"].astype(BF16))
    x2 = _mix_mem(x, ylru, o_cmp, o_slc, o_win, gates, p["g_out_nsa"], p["w_out"], p["g_mem_q"],
                  p["w_mq"], swap(mk), mv, p["w_mo"].astype(BF16))
    return x2.reshape(m, D_MODEL)


def kernel(x, mem, g_mix, w_in, b_gate, conv_w, conv_b, w_rg_a, b_rg_a, w_rg_i, b_rg_i, lam,
           cmp_pos_k, cmp_w1_k, cmp_w2_k, cmp_pos_v, cmp_w1_v, cmp_w2_v, g_out_lru, g_out_nsa,
           w_out, g_mem_q, g_mem_kv, w_mq, w_mk, w_mv, w_mo, g_ffn, w_pq, sub_keys, peer_u,
           peer_v, g_final):
    depth = w_in.shape[0]
    assert depth == 1, "the final norm is fused into the single layer's PEER kernel"
    names = ["g_mix", "w_in", "b_gate", "conv_w", "conv_b", "w_rg_a", "b_rg_a", "w_rg_i", "b_rg_i",
             "lam", "cmp_pos_k", "cmp_w1_k", "cmp_w2_k", "cmp_pos_v", "cmp_w1_v", "cmp_w2_v",
             "g_out_lru", "g_out_nsa", "w_out", "g_mem_q", "g_mem_kv", "w_mq", "w_mk", "w_mv", "w_mo",
             "g_ffn", "w_pq", "sub_keys", "peer_u", "peer_v"]
    vals = [g_mix, w_in, b_gate, conv_w, conv_b, w_rg_a, b_rg_a, w_rg_i, b_rg_i, lam, cmp_pos_k,
            cmp_w1_k, cmp_w2_k, cmp_pos_v, cmp_w1_v, cmp_w2_v, g_out_lru, g_out_nsa, w_out, g_mem_q,
            g_mem_kv, w_mq, w_mk, w_mv, w_mo, g_ffn, w_pq, sub_keys, peer_u, peer_v]
    p = {n: v[0] for n, v in zip(names, vals)}
    bsz, seq, _ = x.shape
    x2 = _layer(x, mem, p)
    xn, r2, e2, ns, sc = _peer_route(x2, p["g_ffn"], p["w_pq"], p["sub_keys"])
    out = _peer_dense(x2, xn, r2, e2, ns, sc, p["peer_u"], p["peer_v"], g_final)
    return out.reshape(bsz, seq, D_MODEL)
```

```python
import functools

import numpy as np
import jax
import jax.numpy as jnp
from jax import lax
from jax.experimental import pallas as pl
from jax.experimental.pallas import tpu as pltpu

F32 = jnp.float32
BF16 = jnp.bfloat16

D_MODEL = 1024
N_MEM = 256
D_LRU = 512
LRU_BLOCKS = 8
CONV_W = 4
LRU_C = 8.0
N_HEADS = 8
N_KV = 2
GROUP = N_HEADS // N_KV
HEAD_DIM = 64
D_NSA = N_HEADS * HEAD_DIM
CMP_BLOCK = 32
CMP_STRIDE = 16
SEL_BLOCK = 64
N_SELECT = 16
WINDOW = 512
MEM_HEADS = 4
MEM_HEAD_DIM = D_MODEL // MEM_HEADS
PEER_HEADS = 8
PEER_NKEYS = 128
PEER_TOPK = 16
PEER_QDIM = 256
EPS = 1e-6
KV_W = N_KV * HEAD_DIM
N_IN = 2 * D_LRU + D_NSA + 6 * KV_W + 3 * N_HEADS
N_IN_PAD = 2 * D_LRU + D_NSA + 6 * KV_W + 128

LANES = 128
SUBLANES = 8
NEG = -1e30
VMEM_LIMIT = 56 * 1024 * 1024

Q_TILE = 256
K_TILE = 1024
K_DIAG = 512


def _cparams(sem):
    return pltpu.CompilerParams(dimension_semantics=sem, vmem_limit_bytes=VMEM_LIMIT)


def _rms(x, g):
    return x * lax.rsqrt(jnp.mean(x * x, axis=-1, keepdims=True) + EPS) * g


def _gelu(x):
    c = 0.7978845608028654
    return 0.5 * x * (1.0 + jnp.tanh(c * (x + 0.044715 * (x * x * x))))


def _dot(a, b):
    return jnp.dot(a, b, preferred_element_type=F32)


def _dot_nt(a, b):
    return lax.dot_general(a, b, (((1,), (1,)), ((), ())), preferred_element_type=F32)


def _split3(a):
    hi = a.astype(BF16)
    r1 = a - hi.astype(F32)
    mid = r1.astype(BF16)
    lo = (r1 - mid.astype(F32)).astype(BF16)
    return hi, mid, lo


def _batcher_pairs(n):
    pairs = []
    p = 1
    while p < n:
        k = p
        while k >= 1:
            for j in range(k % p, n - k, 2 * k):
                for i in range(min(k, n - j - k)):
                    if (i + j) // (2 * p) == (i + j + k) // (2 * p):
                        pairs.append((i + j, i + j + k))
            k //= 2
        p *= 2
    return pairs


_SORT16 = _batcher_pairs(16)


def _sort_desc(xs):
    xs = list(xs)
    for i, j in _batcher_pairs(len(xs)):
        a, b = xs[i], xs[j]
        xs[i], xs[j] = jnp.maximum(a, b), jnp.minimum(a, b)
    return xs


def _bitonic_merge_desc(xs):
    xs = list(xs)
    n = len(xs)
    stride = n // 2
    while stride >= 1:
        for i in range(n):
            if (i & stride) == 0:
                a, b = xs[i], xs[i + stride]
                xs[i], xs[i + stride] = jnp.maximum(a, b), jnp.minimum(a, b)
        stride //= 2
    return xs


def _top_half(a, b):
    n = len(a)
    return [jnp.maximum(a[i], b[n - 1 - i]) for i in range(n)]


def _topk_rows(x, *, want_sorted):
    k = PEER_TOPK
    rows = [x[SUBLANES * v:SUBLANES * (v + 1), :] for v in range(k)]
    rows = _sort_desc(rows)
    for step, shift in enumerate((4, 2, 1)):
        part = [pltpu.roll(r, shift, 0) for r in rows]
        rows = _top_half(rows, part)
        if step < 2 or want_sorted:
            rows = _bitonic_merge_desc(rows)
    if want_sorted:
        return rows
    return functools.reduce(jnp.minimum, rows)


def _inproj_kernel(x_ref, g_ref, w_ref, wkt_ref, bg_ref,
                   xl_ref, gl_ref, q_ref, kvc_ref, ka_ref, va_ref, kw_ref, vw_ref, gate_ref, *, tm):
    s0 = pl.program_id(1) * tm
    h = _rms(x_ref[...], g_ref[...]).astype(BF16)
    z = _dot(h, w_ref[...])
    kt = _dot_nt(wkt_ref[...], h)
    o = 0
    xl_ref[...] = z[:, o:o + D_LRU]; o += D_LRU
    gl_ref[...] = z[:, o:o + D_LRU]; o += D_LRU
    zq = z[:, o:o + D_NSA] * (HEAD_DIM ** -0.5); o += D_NSA
    kvc_ref[...] = z[:, o:o + 2 * KV_W]; o += 2 * KV_W
    zvs = z[:, o:o + KV_W]; o += KV_W
    zvw = z[:, o:o + KV_W]; o += KV_W
    gate_ref[...] = jax.nn.sigmoid(z[:, o:o + LANES] + bg_ref[...])

    lane = lax.broadcasted_iota(jnp.int32, (tm, LANES), 1)
    low = lane < HEAD_DIM
    ones_col = jnp.where(lane == HEAD_DIM, 1.0, 0.0)

    def head_block(src, idx):
        blk = src[:, (idx // 2) * LANES:(idx // 2 + 1) * LANES]
        return pltpu.roll(blk, HEAD_DIM, 1) if idx % 2 else blk

    for hh in range(N_HEADS):
        q_ref[hh // GROUP, hh % GROUP] = jnp.where(low, head_block(zq, hh), 0.0).astype(BF16)
    blk_id = lax.broadcasted_iota(jnp.int32, (LANES, tm), 0)
    tok = s0 + lax.broadcasted_iota(jnp.int32, (LANES, tm), 1)
    onehot = jnp.where(tok // SEL_BLOCK == blk_id, 1.0, 0.0).astype(BF16)
    zeros = jnp.zeros((HEAD_DIM, tm), BF16)
    for kv in range(N_KV):
        va_ref[kv] = jnp.where(low, head_block(zvs, kv), ones_col).astype(BF16)
        vw_ref[kv] = jnp.where(low, head_block(zvw, kv), ones_col).astype(BF16)
        ka_ref[kv, 0:HEAD_DIM, :] = kt[kv * HEAD_DIM:(kv + 1) * HEAD_DIM, :].astype(BF16)
        ka_ref[kv, HEAD_DIM:LANES, :] = zeros
        ka_ref[kv, LANES:2 * LANES, :] = onehot
        kw_ref[kv, 0:HEAD_DIM, :] = kt[KV_W + kv * HEAD_DIM:KV_W + (kv + 1) * HEAD_DIM, :].astype(BF16)
        kw_ref[kv, HEAD_DIM:LANES, :] = zeros


def _in_proj(x, g_mix, w_in, b_gate):
    bsz, seq, _ = x.shape
    tm = 512
    o_k = 2 * D_LRU + D_NSA + 2 * KV_W
    cols = lambda a, n: w_in[:, a:a + n]
    w = jnp.concatenate([cols(0, o_k), cols(o_k + KV_W, KV_W), cols(o_k + 3 * KV_W, KV_W),
                         cols(o_k + 4 * KV_W, 3 * N_HEADS)], axis=1)
    w = jnp.pad(w, ((0, 0), (0, LANES - 3 * N_HEADS))).astype(BF16)
    wkt = jnp.concatenate([cols(o_k, KV_W), cols(o_k + 2 * KV_W, KV_W)], axis=1).T.astype(BF16)
    bg = jnp.pad(b_gate, (0, LANES - 3 * N_HEADS)).reshape(1, LANES)
    row = lambda n: pl.BlockSpec((None, tm, n), lambda b, i: (b, i, 0))
    full = lambda a: pl.BlockSpec(a.shape, lambda b, i: (0,) * a.ndim)
    g = g_mix.reshape(1, D_MODEL)
    sds = jax.ShapeDtypeStruct
    out_shape = [sds((bsz, seq, D_LRU), F32), sds((bsz, seq, D_LRU), F32),
                 sds((bsz, N_KV, GROUP, seq, LANES), BF16), sds((bsz, seq, 2 * KV_W), F32),
                 sds((bsz, N_KV, 2 * LANES, seq), BF16), sds((bsz, N_KV, seq, LANES), BF16),
                 sds((bsz, N_KV, LANES, seq), BF16), sds((bsz, N_KV, seq, LANES), BF16),
                 sds((bsz, seq, LANES), F32)]
    keys_t = lambda n: pl.BlockSpec((None, N_KV, n, tm), lambda b, i: (b, 0, 0, i))
    vals = pl.BlockSpec((None, N_KV, tm, LANES), lambda b, i: (b, 0, i, 0))
    out_specs = [row(D_LRU), row(D_LRU),
                 pl.BlockSpec((None, N_KV, GROUP, tm, LANES), lambda b, i: (b, 0, 0, i, 0)),
                 row(2 * KV_W), keys_t(2 * LANES), vals, keys_t(LANES), vals, row(LANES)]
    return pl.pallas_call(
        functools.partial(_inproj_kernel, tm=tm),
        grid=(bsz, seq // tm),
        in_specs=[row(D_MODEL), full(g), full(w), full(wkt), full(bg)],
        out_specs=out_specs,
        out_shape=out_shape,
        compiler_params=_cparams(("parallel", "parallel")),
        name="in_proj",
    )(x, g, w, wkt, bg)


def _rglru_kernel(x_ref, xprev_ref, gate_ref, cw_ref, cb_ref, wa_ref, ba_ref, wi_ref, bi_ref,
                  lam_ref, gout_ref, o_ref, h_ref, *, t):
    i = pl.program_id(1)

    @pl.when(i == 0)
    def _():
        h_ref[...] = jnp.zeros_like(h_ref)

    x = x_ref[...]
    prev = jnp.where(i > 0, xprev_ref[...], 0.0)
    xs = jnp.concatenate([prev, x], axis=0)
    cw = cw_ref[...]
    xc = cb_ref[...] + x * cw[CONV_W - 1:CONV_W, :]
    for k in range(1, CONV_W):
        xk = pltpu.roll(xs, k, 0)[SUBLANES:, :]
        xc = xc + xk * cw[CONV_W - 1 - k:CONV_W - k, :]
    xcb = xc.astype(BF16)
    r = jax.nn.sigmoid(_dot(xcb, wa_ref[...]) + ba_ref[...])
    gi = jax.nn.sigmoid(_dot(xcb, wi_ref[...]) + bi_ref[...])
    lam = lam_ref[...]
    softplus = jnp.maximum(-lam, 0.0) + jnp.log1p(jnp.exp(-jnp.abs(lam)))
    log_a = (-LRU_C) * r * softplus
    a = jnp.exp(log_a)
    b = jnp.sqrt(-jnp.tanh(log_a) * (a * a + 1.0)) * (gi * xc)
    rowi = lax.broadcasted_iota(jnp.int32, (t, D_LRU), 0)
    d = 1
    while d < t:
        ok = rowi >= d
        a_s = pltpu.roll(a, d, 0)
        b_s = pltpu.roll(b, d, 0)
        b = jnp.where(ok, a * b_s, 0.0) + b
        a = jnp.where(ok, a * a_s, a)
        d *= 2
    h = b + a * h_ref[0:1, :]
    h_ref[...] = jnp.broadcast_to(h[t - 1:t, :], h_ref.shape)
    y = h * _gelu(gate_ref[...])
    o_ref[...] = _rms(y, gout_ref[...]).astype(o_ref.dtype)


def _block_diag(w):
    nb, bd, _ = w.shape
    eye = jnp.eye(nb, dtype=w.dtype)
    return (eye[:, None, :, None] * w[:, :, None, :]).reshape(nb * bd, nb * bd)


def _rg_lru(xl, gl, conv_w, conv_b, w_a, b_a, w_i, b_i, lam, g_out, bsz, seq):
    t = 256
    xl3 = xl.reshape(bsz, seq, D_LRU)
    gl3 = gl.reshape(bsz, seq, D_LRU)
    cw = jnp.pad(conv_w, ((0, SUBLANES - CONV_W), (0, 0)))
    vec = lambda a: a.reshape(1, D_LRU)
    wa = _block_diag(w_a).astype(BF16)
    wi = _block_diag(w_i).astype(BF16)
    args = [cw, vec(conv_b), wa, vec(b_a), wi, vec(b_i), vec(lam), vec(g_out)]
    full = lambda a: pl.BlockSpec(a.shape, lambda b, i: (0,) * a.ndim)
    tile = pl.BlockSpec((None, t, D_LRU), lambda b, i: (b, i, 0))
    prev = pl.BlockSpec((None, SUBLANES, D_LRU),
                        lambda b, i: (b, jnp.maximum(i * (t // SUBLANES) - 1, 0), 0))
    return pl.pallas_call(
        functools.partial(_rglru_kernel, t=t),
        grid=(bsz, seq // t),
        in_specs=[tile, prev, tile] + [full(a) for a in args],
        out_specs=tile,
        out_shape=jax.ShapeDtypeStruct((bsz, seq, D_LRU), BF16),
        scratch_shapes=[pltpu.VMEM((SUBLANES, D_LRU), F32)],
        compiler_params=_cparams(("parallel", "arbitrary")),
        name="rg_lru",
    )(xl3, xl3, gl3, *args)


def _compress_kernel(r_ref, pa_ref, pb_ref, w1a_ref, w1b_ref, w2_ref, k_ref, v_ref):
    r = r_ref[...]
    n = r.shape[0]

    def dot3(x, w_ref):
        out = None
        for piece in _split3(x):
            for wi in range(3):
                term = _dot(piece, w_ref[wi])
                out = term if out is None else out + term
        return out

    first = dot3(r + pa_ref[...], w1a_ref)
    second = dot3(r + pb_ref[...], w1b_ref)
    pre = first + pltpu.roll(second, n - 1, 0)
    out = dot3(_gelu(pre), w2_ref)
    k_ref[...] = out[:, :KV_W]
    v_ref[...] = out[:, KV_W:].astype(v_ref.dtype)


def _compress(kvc, pos_k, w1_k, w2_k, pos_v, w1_v, w2_v, bsz, seq):
    nrow = seq // CMP_STRIDE
    width = CMP_STRIDE * 2 * KV_W
    r = kvc.reshape(bsz, nrow, width)
    eye = jnp.eye(N_KV, dtype=F32)

    def big1(w1, half):
        w = w1.reshape(CMP_BLOCK, HEAD_DIM, HEAD_DIM)[half * CMP_STRIDE:(half + 1) * CMP_STRIDE]
        return jnp.einsum("ldo,hg->lhdgo", w, eye).reshape(CMP_STRIDE, KV_W, KV_W)

    def both1(half):
        z = jnp.zeros((CMP_STRIDE, KV_W, KV_W), F32)
        top = jnp.concatenate([big1(w1_k, half), z], axis=2)
        bot = jnp.concatenate([z, big1(w1_v, half)], axis=2)
        return jnp.concatenate([top, bot], axis=1).reshape(width, 2 * KV_W)

    def pos_row(half):
        pk = jnp.tile(pos_k[half * CMP_STRIDE:(half + 1) * CMP_STRIDE, None, :], (1, N_KV, 1))
        pv = jnp.tile(pos_v[half * CMP_STRIDE:(half + 1) * CMP_STRIDE, None, :], (1, N_KV, 1))
        return jnp.concatenate([pk, pv], axis=1).reshape(1, width)

    def w2big():
        zk = jnp.zeros((KV_W, KV_W), F32)
        k2 = jnp.einsum("do,hg->hdgo", w2_k, eye).reshape(KV_W, KV_W)
        v2 = jnp.einsum("do,hg->hdgo", w2_v, eye).reshape(KV_W, KV_W)
        return jnp.concatenate([jnp.concatenate([k2, zk], 1), jnp.concatenate([zk, v2], 1)], 0)

    pieces = lambda w: jnp.stack(_split3(w))
    args = [pos_row(0), pos_row(1), pieces(both1(0)), pieces(both1(1)), pieces(w2big())]
    full = lambda a: pl.BlockSpec(a.shape, lambda b: (0,) * a.ndim)
    return pl.pallas_call(
        _compress_kernel,
        grid=(bsz,),
        in_specs=[pl.BlockSpec((None, nrow, width), lambda b: (b, 0, 0))] + [full(a) for a in args],
        out_specs=[pl.BlockSpec((None, nrow, KV_W), lambda b: (b, 0, 0))] * 2,
        out_shape=[jax.ShapeDtypeStruct((bsz, nrow, KV_W), F32),
                   jax.ShapeDtypeStruct((bsz, nrow, KV_W), BF16)],
        compiler_params=_cparams(("parallel",)),
        name="compress",
    )(r, *args)


def _heads_to_lanes(o):
    return jnp.concatenate([o[g * Q_TILE:(g + 1) * Q_TILE, :] for g in range(GROUP)], axis=1)


def _select_kernel(q_ref, kc_ref, vc_ref, ovl_ref, tri_ref, o_ref, bias_ref, *, nsb):
    s0 = pl.program_id(2) * Q_TILE
    rows = GROUP * Q_TILE
    ncp = kc_ref.shape[1]
    q = q_ref[...].reshape(rows, LANES)
    kc_hi, kc_mid, _ = _split3(kc_ref[...])
    s = _dot(q, kc_hi) + _dot(q, kc_mid)
    t = s0 + (lax.broadcasted_iota(jnp.int32, (rows, ncp), 0) & (Q_TILE - 1))
    cend = lax.broadcasted_iota(jnp.int32, (rows, ncp), 1) * CMP_STRIDE + (CMP_BLOCK - 1)
    mask = cend <= t
    sm = jnp.where(mask, s, NEG)
    mx = jnp.max(sm, axis=1, keepdims=True)
    p = jnp.where(mask, jnp.exp(sm - mx), 0.0)
    l = jnp.sum(p, axis=1, keepdims=True)
    pc = p / jnp.maximum(l, 1e-30)
    o = _dot(pc.astype(BF16), vc_ref[...])
    o_ref[...] = _heads_to_lanes(o)
    psum = pc[0:Q_TILE]
    for g in range(1, GROUP):
        psum = psum + pc[g * Q_TILE:(g + 1) * Q_TILE]
    ovl = ovl_ref[...]
    imp = None
    for piece in _split3(psum):
        term = _dot(piece, ovl)
        imp = term if imp is None else imp + term
    tq = s0 + lax.broadcasted_iota(jnp.int32, (Q_TILE, nsb), 0)
    jb = lax.broadcasted_iota(jnp.int32, (Q_TILE, nsb), 1)
    cur = tq // SEL_BLOCK
    forced = (jb == 0) | (jb == cur) | (jb == cur - 1)
    val = jnp.where(jb <= cur, jnp.where(forced, jnp.inf, imp), -jnp.inf)
    vt = val.T
    tau = _topk_rows(vt, want_sorted=False)[0:1, :]
    gt = vt > tau
    tie = vt == tau
    need = float(N_SELECT) - jnp.sum(gt.astype(F32), axis=0, keepdims=True)
    pre = _dot(tri_ref[...], tie.astype(BF16))
    sel = (gt | (tie & (pre <= need))) & (vt > -1.0)
    bias_ref[...] = jnp.where(sel, 0.0, NEG).T.astype(bias_ref.dtype)


def _nsa_select(q4, kc4, vc4, bsz, seq):
    assert seq // SEL_BLOCK <= LANES, "block-selection kernel holds at most 128 selection blocks"
    nsb = LANES
    ncp = kc4.shape[3]
    ci = np.arange(ncp)[:, None] * CMP_STRIDE
    bj = np.arange(nsb)[None, :] * SEL_BLOCK
    ovl = ((ci < bj + SEL_BLOCK) & (ci + CMP_BLOCK > bj) & (np.arange(ncp)[:, None] < ncp - 1))
    ovl = jnp.asarray(ovl, BF16)
    tri = jnp.asarray(np.tril(np.ones((nsb, nsb))), BF16)
    full = lambda a: pl.BlockSpec(a.shape, lambda b, k, i: (0,) * a.ndim)
    return pl.pallas_call(
        functools.partial(_select_kernel, nsb=nsb),
        grid=(bsz, N_KV, seq // Q_TILE),
        in_specs=[pl.BlockSpec((None, None, GROUP, Q_TILE, LANES), lambda b, k, i: (b, k, 0, i, 0)),
                  pl.BlockSpec((None, None, LANES, ncp), lambda b, k, i: (b, k, 0, 0)),
                  pl.BlockSpec((None, None, ncp, HEAD_DIM), lambda b, k, i: (b, k, 0, 0)),
                  full(ovl), full(tri)],
        out_specs=[pl.BlockSpec((None, Q_TILE, GROUP * HEAD_DIM), lambda b, k, i: (b, i, k)),
                   pl.BlockSpec((None, None, Q_TILE, nsb), lambda b, k, i: (b, k, i, 0))],
        out_shape=[jax.ShapeDtypeStruct((bsz, seq, D_NSA), F32),
                   jax.ShapeDtypeStruct((bsz, N_KV, seq, nsb), BF16)],
        compiler_params=_cparams(("parallel", "parallel", "parallel")),
        name="nsa_select",
    )(q4, kc4, vc4, ovl, tri)


def _attend_kernel(q_ref, bias_ref, ka_ref, va_ref, kw_ref, vw_ref, os_ref, ow_ref):
    s0 = pl.program_id(2) * Q_TILE
    rows = GROUP * Q_TILE
    q = q_ref[...].reshape(rows, LANES)
    bias = bias_ref[...]
    qa = jnp.concatenate([q, jnp.concatenate([bias] * GROUP, axis=0)], axis=1)
    t_row = s0 + (lax.broadcasted_iota(jnp.int32, (rows, 1), 0) & (Q_TILE - 1))

    def step(tile, width, carry, masked):
        m, acc = carry
        k0 = pl.multiple_of(tile * width, width)
        s = _dot(qa, ka_ref[:, pl.ds(k0, width)])
        if masked:
            kpos = k0 + lax.broadcasted_iota(jnp.int32, (rows, width), 1)
            s = jnp.where(kpos <= t_row, s, NEG)
        mn = jnp.maximum(m, jnp.max(s, axis=1, keepdims=True))
        p = jnp.exp(s - mn)
        acc = jnp.exp(m - mn) * acc + _dot(p.astype(BF16), va_ref[pl.ds(k0, width), :])
        return mn, acc

    n_diag = s0 // K_DIAG
    n_wide = s0 // K_TILE
    init = (jnp.full((rows, 1), NEG, F32), jnp.zeros((rows, LANES), F32))
    carry = lax.fori_loop(0, n_wide, lambda kt, c: step(kt, K_TILE, c, False), init)
    carry = lax.fori_loop(n_wide * (K_TILE // K_DIAG), n_diag,
                          lambda kt, c: step(kt, K_DIAG, c, False), carry)
    _, acc = step(n_diag, K_DIAG, carry, True)
    os_ref[...] = _heads_to_lanes(acc[:, :HEAD_DIM] / acc[:, HEAD_DIM:HEAD_DIM + 1])

    span = WINDOW + Q_TILE
    w0 = pl.multiple_of(jnp.maximum(s0 - WINDOW, 0), Q_TILE)
    s = _dot(q, kw_ref[:, pl.ds(w0, span)])
    kpos = w0 + lax.broadcasted_iota(jnp.int32, (rows, span), 1)
    mask = (kpos <= t_row) & (kpos > t_row - WINDOW)
    s = jnp.where(mask, s, NEG)
    p = jnp.where(mask, jnp.exp(s - jnp.max(s, axis=1, keepdims=True)), 0.0)
    acc = _dot(p.astype(BF16), vw_ref[pl.ds(w0, span), :])
    ow_ref[...] = _heads_to_lanes(acc[:, :HEAD_DIM] / acc[:, HEAD_DIM:HEAD_DIM + 1])


def _nsa_attend(q4, bias, ka, va, kw, vw, bsz, seq):
    per_head = lambda n: pl.BlockSpec((None, None, seq, n), lambda b, k, i: (b, k, 0, 0))
    per_head_t = lambda n: pl.BlockSpec((None, None, n, seq), lambda b, k, i: (b, k, 0, 0))
    out = pl.BlockSpec((None, Q_TILE, GROUP * HEAD_DIM), lambda b, k, i: (b, i, k))
    return pl.pallas_call(
        _attend_kernel,
        grid=(bsz, N_KV, seq // Q_TILE),
        in_specs=[pl.BlockSpec((None, None, GROUP, Q_TILE, LANES), lambda b, k, i: (b, k, 0, i, 0)),
                  pl.BlockSpec((None, None, Q_TILE, LANES), lambda b, k, i: (b, k, i, 0)),
                  per_head_t(2 * LANES), per_head(LANES), per_head_t(LANES), per_head(LANES)],
        out_specs=[out, out],
        out_shape=[jax.ShapeDtypeStruct((bsz, seq, D_NSA), F32)] * 2,
        compiler_params=_cparams(("parallel", "parallel", "arbitrary")),
        name="nsa_attend",
    )(q4, bias, ka, va, kw, vw)


def _memkv_kernel(m_ref, g_ref, wk_ref, wv_ref, k_ref, v_ref):
    mn = _rms(m_ref[...], g_ref[...]).astype(BF16)
    k_ref[...] = _dot(mn, wk_ref[...]).astype(k_ref.dtype)
    v_ref[...] = _dot(mn, wv_ref[...]).astype(v_ref.dtype)


def _mem_kv(mem, g, wk, wv):
    bsz, nm, _ = mem.shape
    g2 = g.reshape(1, D_MODEL)
    full = lambda a: pl.BlockSpec(a.shape, lambda b: (0,) * a.ndim)
    blk = pl.BlockSpec((None, nm, D_MODEL), lambda b: (b, 0, 0))
    return pl.pallas_call(
        _memkv_kernel,
        grid=(bsz,),
        in_specs=[blk, full(g2), full(wk), full(wv)],
        out_specs=[blk, blk],
        out_shape=[jax.ShapeDtypeStruct(mem.shape, BF16)] * 2,
        compiler_params=_cparams(("parallel",)),
        name="mem_kv",
    )(mem, g2, wk, wv)


def _mix_kernel(x_ref, ylru_ref, oc_ref, os_ref, ow_ref, gate_ref, e_ref, gnsa_ref, wout_ref,
                gq_ref, wq_ref, mk_ref, mv_ref, wo_ref, o_ref):
    g = gate_ref[...]
    g_hi = g.astype(BF16)
    g_lo = (g - g_hi.astype(F32)).astype(BF16)
    e = e_ref[...]
    gx = _dot(g_hi, e) + _dot(g_lo, e)
    y = (gx[:, 0:D_NSA] * oc_ref[...] + gx[:, D_NSA:2 * D_NSA] * os_ref[...]
         + gx[:, 2 * D_NSA:3 * D_NSA] * ow_ref[...])
    yn = _rms(y, gnsa_ref[...]).astype(BF16)
    ycat = jnp.concatenate([ylru_ref[...], yn], axis=1)
    x1 = x_ref[...] + _dot(ycat, wout_ref[...])
    xq = _rms(x1, gq_ref[...]).astype(BF16)
    qm = _dot(xq, wq_ref[...]).astype(BF16)
    outs = []
    for h in range(MEM_HEADS):
        sl = slice(h * MEM_HEAD_DIM, (h + 1) * MEM_HEAD_DIM)
        s = _dot(qm[:, sl], mk_ref[sl, :])
        p = jnp.exp(s - jnp.max(s, axis=1, keepdims=True))
        p = p / jnp.sum(p, axis=1, keepdims=True)
        outs.append(_dot(p.astype(BF16), mv_ref[:, sl]))
    o = jnp.concatenate(outs, axis=1).astype(BF16)
    o_ref[...] = x1 + _dot(o, wo_ref[...])


def _mix_mem(x3, ylru, oc, osl, ow, gates, g_nsa, w_out, g_q, w_mq, mk, mv, w_mo):
    bsz, seq, _ = x3.shape
    tm = 256
    e = np.zeros((LANES, 3 * D_NSA), np.float32)
    for h in range(N_HEADS):
        for j in range(3):
            e[h * 3 + j, j * D_NSA + h * HEAD_DIM:j * D_NSA + (h + 1) * HEAD_DIM] = 1.0
    e = jnp.asarray(e, BF16)
    vec = lambda a: a.reshape(1, -1)
    wq = (w_mq * (MEM_HEAD_DIM ** -0.5)).astype(BF16)
    row = lambda n: pl.BlockSpec((None, tm, n), lambda b, i: (b, i, 0))
    full = lambda a: pl.BlockSpec(a.shape, lambda b, i: (0,) * a.ndim)
    memb = pl.BlockSpec((None, mv.shape[1], D_MODEL), lambda b, i: (b, 0, 0))
    membt = pl.BlockSpec((None, D_MODEL, mv.shape[1]), lambda b, i: (b, 0, 0))
    consts = [e, vec(g_nsa), w_out.astype(BF16), vec(g_q), wq]
    return pl.pallas_call(
        _mix_kernel,
        grid=(bsz, seq // tm),
        in_specs=[row(D_MODEL), row(D_LRU), row(D_NSA), row(D_NSA), row(D_NSA), row(LANES)]
        + [full(a) for a in consts] + [membt, memb, full(w_mo)],
        out_specs=row(D_MODEL),
        out_shape=jax.ShapeDtypeStruct(x3.shape, F32),
        compiler_params=_cparams(("parallel", "parallel")),
        name="mix_mem",
    )(x3, ylru, oc, osl, ow, gates.reshape(bsz, seq, LANES), *consts, mk, mv, w_mo)


def _prefix_count(rows, test):
    pick = jnp.where
    b3 = test(rows[7])
    b2 = test(pick(b3, rows[11], rows[3]))
    b1 = test(pick(b3, pick(b2, rows[13], rows[9]), pick(b2, rows[5], rows[1])))
    b0 = test(pick(b3, pick(b2, pick(b1, rows[14], rows[12]), pick(b1, rows[10], rows[8])),
                   pick(b2, pick(b1, rows[6], rows[4]), pick(b1, rows[2], rows[0]))))
    cnt = pick(b3, 8.0, 0.0) + pick(b2, 4.0, 0.0) + pick(b1, 2.0, 0.0) + pick(b0, 1.0, 0.0)
    return pick(test(rows[15]), 16.0, cnt)


def _peer_pairs():
    k = PEER_TOPK
    return [(a, b) for a in range(k) for b in range(k) if (a + 1) * (b + 1) <= k]


def _route_kernel(x_ref, g_ref, wpq_ref, sk_ref, xn_ref, r2_ref, e2_ref, ns_ref, sc_ref,
                  q_scr, s_scr, l_scr, t_scr, *, tt):
    k = PEER_TOPK
    nhc = 2 * PEER_HEADS
    xn32 = _rms(x_ref[...], g_ref[...])
    xn = xn32.astype(BF16)
    xn_ref[...] = xn32.T.astype(BF16)
    q = _dot(xn, wpq_ref[...]).astype(BF16)
    for hc in range(nhc):
        q_scr[hc] = q[:, hc * PEER_NKEYS:(hc + 1) * PEER_NKEYS]

    def score_and_sort(hc, carry):
        st = _dot_nt(sk_ref[hc], q_scr[hc])
        s_scr[hc] = st
        rows = _topk_rows(st, want_sorted=True)
        h, c = hc // 2, hc % 2
        for a in range(k):
            l_scr[c, a, pl.ds(h, 1), :] = rows[a][0:1, :]
        return carry

    lax.fori_loop(0, nhc, score_and_sort, 0)
    v1 = [l_scr[0, a] for a in range(k)]
    v2 = [l_scr[1, b] for b in range(k)]
    pairs = _peer_pairs()
    cand = [v1[a] + v2[b] for a, b in pairs]
    pad = [jnp.full_like(cand[0], -jnp.inf)] * (4 * k - len(cand))
    groups = [_sort_desc((cand + pad)[i * k:(i + 1) * k]) for i in range(4)]
    left = _bitonic_merge_desc(_top_half(groups[0], groups[1]))
    right = _bitonic_merge_desc(_top_half(groups[2], groups[3]))
    tau = functools.reduce(jnp.minimum, _top_half(left, right))
    top = v1[0] + v2[0]
    z = jnp.zeros_like(top)
    for c_ in cand:
        z = z + jnp.where(c_ >= tau, jnp.exp(c_ - top), 0.0)
    t_scr[0] = tau
    t_scr[1] = 1.0 / z

    def per_head(h, carry):
        s1, s2 = s_scr[2 * h], s_scr[2 * h + 1]
        tau_h = t_scr[0, pl.ds(h, 1), :]
        zinv_h = t_scr[1, pl.ds(h, 1), :]
        l2 = [l_scr[1, b, pl.ds(h, 1), :] for b in range(k)]
        r2_ref[h] = _prefix_count(l2, lambda v: v > s2).astype(r2_ref.dtype)
        ns_ref[h] = _prefix_count(l2, lambda v: (s1 + v) >= tau_h)
        e2_ref[h] = jnp.exp(s2 - l_scr[1, 0, pl.ds(h, 1), :]).astype(e2_ref.dtype)
        sc_ref[h] = jnp.exp(s1 - l_scr[0, 0, pl.ds(h, 1), :]) * zinv_h
        return carry

    lax.fori_loop(0, PEER_HEADS, per_head, 0)


def _peer_route(x2d, g_ffn, w_pq, sub_keys):
    m = x2d.shape[0]
    tt = min(512, m)
    g = g_ffn.reshape(1, D_MODEL)
    wpq = w_pq.astype(BF16)
    sk = sub_keys.astype(BF16).reshape(2 * PEER_HEADS, PEER_NKEYS, PEER_QDIM // 2)
    full = lambda a: pl.BlockSpec(a.shape, lambda i: (0,) * a.ndim)
    tcol = pl.BlockSpec((PEER_HEADS, PEER_NKEYS, tt), lambda i: (0, 0, i))
    shp = lambda dt: jax.ShapeDtypeStruct((PEER_HEADS, PEER_NKEYS, m), dt)
    return pl.pallas_call(
        functools.partial(_route_kernel, tt=tt),
        grid=(m // tt,),
        in_specs=[pl.BlockSpec((tt, D_MODEL), lambda i: (i, 0)), full(g), full(wpq), full(sk)],
        out_specs=[pl.BlockSpec((D_MODEL, tt), lambda i: (0, i)), tcol, tcol, tcol, tcol],
        out_shape=[jax.ShapeDtypeStruct((D_MODEL, m), BF16), shp(BF16), shp(BF16), shp(F32), shp(F32)],
        scratch_shapes=[pltpu.VMEM((2 * PEER_HEADS, tt, PEER_NKEYS), BF16),
                        pltpu.VMEM((2 * PEER_HEADS, PEER_NKEYS, tt), F32),
                        pltpu.VMEM((2, PEER_TOPK, PEER_HEADS, tt), F32),
                        pltpu.VMEM((2, PEER_HEADS, tt), F32)],
        compiler_params=_cparams(("parallel",)),
        name="peer_route",
    )(x2d, g, wpq, sk)


def _dense_kernel(x_ref, xn_ref, r2_ref, e2_ref, ns_ref, sc_ref, u_ref, vt_ref, gf_ref,
                  o_ref, acc_ref, a_ref, *, i1b, tt):
    ei = pl.program_id(1)

    @pl.when(ei == 0)
    def _():
        acc_ref[...] = jnp.zeros_like(acc_ref)

    gh = _gelu(_dot(u_ref[...], xn_ref[...]).astype(BF16))
    for kk in range(i1b):
        i1 = ei * i1b + kk
        w = None
        for h in range(PEER_HEADS):
            n_row = ns_ref[h, pl.ds(i1, 1), :].astype(BF16)
            s_row = sc_ref[h, pl.ds(i1, 1), :].astype(BF16)
            term = jnp.where(r2_ref[h] < n_row, e2_ref[h], jnp.zeros((), BF16)) * s_row
            w = term if w is None else w + term
        rs = slice(kk * PEER_NKEYS, (kk + 1) * PEER_NKEYS)
        a_ref[rs, :] = w * gh[rs, :]
    acc_ref[...] += _dot(vt_ref[...], a_ref[...])

    @pl.when(ei == pl.num_programs(1) - 1)
    def _():
        o_ref[...] = _rms(x_ref[...] + acc_ref[...].T, gf_ref[...])


def _peer_dense(x2d, xn, r2, e2, ns, sc, peer_u, peer_v, g_final):
    m = x2d.shape[0]
    tt = min(512, m)
    i1b = 8
    te = i1b * PEER_NKEYS
    u = peer_u.astype(BF16)
    vt = peer_v.T.astype(BF16)
    gf = g_final.reshape(1, D_MODEL)
    tok = pl.BlockSpec((tt, D_MODEL), lambda t, e: (t, 0))
    tcol = pl.BlockSpec((PEER_HEADS, PEER_NKEYS, tt), lambda t, e: (0, 0, t))
    return pl.pallas_call(
        functools.partial(_dense_kernel, i1b=i1b, tt=tt),
        grid=(m // tt, PEER_NKEYS // i1b),
        in_specs=[tok, pl.BlockSpec((D_MODEL, tt), lambda t, e: (0, t)), tcol, tcol, tcol, tcol,
                  pl.BlockSpec((te, D_MODEL), lambda t, e: (e, 0)),
                  pl.BlockSpec((D_MODEL, te), lambda t, e: (0, e)),
                  pl.BlockSpec((1, D_MODEL), lambda t, e: (0, 0))],
        out_specs=tok,
        out_shape=jax.ShapeDtypeStruct((m, D_MODEL), F32),
        scratch_shapes=[pltpu.VMEM((D_MODEL, tt), F32), pltpu.VMEM((te, tt), BF16)],
        compiler_params=_cparams(("parallel", "arbitrary")),
        name="peer_dense",
    )(x2d, xn, r2, e2, ns, sc, u, vt, gf)


def _per_kv_head(a, bsz, seq):
    return a.reshape(bsz, seq, N_KV, HEAD_DIM).transpose(0, 2, 1, 3)


def _layer(x, mem, p):
    bsz, seq, _ = x.shape
    m = bsz * seq
    xl, gl, q4, kvc, ka, va, kw, vw, gates = _in_proj(x, p["g_mix"], p["w_in"], p["b_gate"])
    ylru = _rg_lru(xl, gl, p["conv_w"], p["conv_b"], p["w_rg_a"], p["b_rg_a"], p["w_rg_i"],
                   p["b_rg_i"], p["lam"], p["g_out_lru"], bsz, seq)
    kc, vc = _compress(kvc, p["cmp_pos_k"], p["cmp_w1_k"], p["cmp_w2_k"], p["cmp_pos_v"],
                       p["cmp_w1_v"], p["cmp_w2_v"], bsz, seq)
    ncp = seq // CMP_STRIDE
    pad64 = lambda a: jnp.pad(a, ((0, 0),) * (a.ndim - 1) + ((0, LANES - HEAD_DIM),))
    swap = lambda a: jnp.swapaxes(a, -1, -2)
    kc4 = swap(pad64(kc.reshape(bsz, ncp, N_KV, HEAD_DIM).transpose(0, 2, 1, 3)))
    vc4 = vc.reshape(bsz, ncp, N_KV, HEAD_DIM).transpose(0, 2, 1, 3)
    o_cmp, bias = _nsa_select(q4, kc4, vc4, bsz, seq)
    o_slc, o_win = _nsa_attend(q4, bias, ka, va, kw, vw, bsz, seq)

    mk, mv = _mem_kv(mem, p["g_mem_kv"], p["w_mk"].astype(BF16), p["w_mv"].astype(BF16))
    x2 = _mix_mem(x, ylru, o_cmp, o_slc, o_win, gates, p["g_out_nsa"], p["w_out"], p["g_mem_q"],
                  p["w_mq"], swap(mk), mv, p["w_mo"].astype(BF16))
    return x2.reshape(m, D_MODEL)


def kernel(x, mem, g_mix, w_in, b_gate, conv_w, conv_b, w_rg_a, b_rg_a, w_rg_i, b_rg_i, lam,
           cmp_pos_k, cmp_w1_k, cmp_w2_k, cmp_pos_v, cmp_w1_v, cmp_w2_v, g_out_lru, g_out_nsa,
           w_out, g_mem_q, g_mem_kv, w_mq, w_mk, w_mv, w_mo, g_ffn, w_pq, sub_keys, peer_u,
           peer_v, g_final):
    depth = w_in.shape[0]
    assert depth == 1, "the final norm is fused into the single layer's PEER kernel"
    names = ["g_mix", "w_in", "b_gate", "conv_w", "conv_b", "w_rg_a", "b_rg_a", "w_rg_i", "b_rg_i",
             "lam", "cmp_pos_k", "cmp_w1_k", "cmp_w2_k", "cmp_pos_v", "cmp_w1_v", "cmp_w2_v",
             "g_out_lru", "g_out_nsa", "w_out", "g_mem_q", "g_mem_kv", "w_mq", "w_mk", "w_mv", "w_mo",
             "g_ffn", "w_pq", "sub_keys", "peer_u", "peer_v"]
    vals = [g_mix, w_in, b_gate, conv_w, conv_b, w_rg_a, b_rg_a, w_rg_i, b_rg_i, lam, cmp_pos_k,
            cmp_w1_k, cmp_w2_k, cmp_pos_v, cmp_w1_v, cmp_w2_v, g_out_lru, g_out_nsa, w_out, g_mem_q,
            g_mem_kv, w_mq, w_mk, w_mv, w_mo, g_ffn, w_pq, sub_keys, peer_u, peer_v]
    p = {n: v[0] for n, v in zip(names, vals)}
    bsz, seq, _ = x.shape
    x2 = _layer(x, mem, p)
    xn, r2, e2, ns, sc = _peer_route(x2, p["g_ffn"], p["w_pq"], p["sub_keys"])
    out = _peer_dense(x2, xn, r2, e2, ns, sc, p["peer_u"], p["peer_v"], g_final)
    return out.reshape(bsz, seq, D_MODEL)
```

```python
import functools

import numpy as np
import jax
import jax.numpy as jnp
from jax import lax
from jax.experimental import pallas as pl
from jax.experimental.pallas import tpu as pltpu

F32 = jnp.float32
BF16 = jnp.bfloat16

D_MODEL = 1024
N_MEM = 256
D_LRU = 512
LRU_BLOCKS = 8
CONV_W = 4
LRU_C = 8.0
N_HEADS = 8
N_KV = 2
GROUP = N_HEADS // N_KV
HEAD_DIM = 64
D_NSA = N_HEADS * HEAD_DIM
CMP_BLOCK = 32
CMP_STRIDE = 16
SEL_BLOCK = 64
N_SELECT = 16
WINDOW = 512
MEM_HEADS = 4
MEM_HEAD_DIM = D_MODEL // MEM_HEADS
PEER_HEADS = 8
PEER_NKEYS = 128
PEER_TOPK = 16
PEER_QDIM = 256
EPS = 1e-6
KV_W = N_KV * HEAD_DIM
N_IN = 2 * D_LRU + D_NSA + 6 * KV_W + 3 * N_HEADS
N_IN_PAD = 2 * D_LRU + D_NSA + 6 * KV_W + 128

LANES = 128
SUBLANES = 8
NEG = -1e30
VMEM_LIMIT = 56 * 1024 * 1024

Q_TILE = 256
K_TILE = 1024
K_DIAG = 512


def _cparams(sem):
    return pltpu.CompilerParams(dimension_semantics=sem, vmem_limit_bytes=VMEM_LIMIT)


def _rms(x, g):
    return x * lax.rsqrt(jnp.mean(x * x, axis=-1, keepdims=True) + EPS) * g


def _gelu(x):
    c = 0.7978845608028654
    return 0.5 * x * (1.0 + jnp.tanh(c * (x + 0.044715 * (x * x * x))))


def _dot(a, b):
    return jnp.dot(a, b, preferred_element_type=F32)


def _dot_nt(a, b):
    return lax.dot_general(a, b, (((1,), (1,)), ((), ())), preferred_element_type=F32)


def _split3(a):
    hi = a.astype(BF16)
    r1 = a - hi.astype(F32)
    mid = r1.astype(BF16)
    lo = (r1 - mid.astype(F32)).astype(BF16)
    return hi, mid, lo


def _batcher_pairs(n):
    pairs = []
    p = 1
    while p < n:
        k = p
        while k >= 1:
            for j in range(k % p, n - k, 2 * k):
                for i in range(min(k, n - j - k)):
                    if (i + j) // (2 * p) == (i + j + k) // (2 * p):
                        pairs.append((i + j, i + j + k))
            k //= 2
        p *= 2
    return pairs


_SORT16 = _batcher_pairs(16)


def _sort_desc(xs):
    xs = list(xs)
    for i, j in _batcher_pairs(len(xs)):
        a, b = xs[i], xs[j]
        xs[i], xs[j] = jnp.maximum(a, b), jnp.minimum(a, b)
    return xs


def _bitonic_merge_desc(xs):
    xs = list(xs)
    n = len(xs)
    stride = n // 2
    while stride >= 1:
        for i in range(n):
            if (i & stride) == 0:
                a, b = xs[i], xs[i + stride]
                xs[i], xs[i + stride] = jnp.maximum(a, b), jnp.minimum(a, b)
        stride //= 2
    return xs


def _top_half(a, b):
    n = len(a)
    return [jnp.maximum(a[i], b[n - 1 - i]) for i in range(n)]


def _topk_rows(x, *, want_sorted):
    k = PEER_TOPK
    rows = [x[SUBLANES * v:SUBLANES * (v + 1), :] for v in range(k)]
    rows = _sort_desc(rows)
    for step, shift in enumerate((4, 2, 1)):
        part = [pltpu.roll(r, shift, 0) for r in rows]
        rows = _top_half(rows, part)
        if step < 2 or want_sorted:
            rows = _bitonic_merge_desc(rows)
    if want_sorted:
        return rows
    return functools.reduce(jnp.minimum, rows)


def _inproj_kernel(x_ref, g_ref, w_ref, wkt_ref, bg_ref,
                   xl_ref, gl_ref, q_ref, kvc_ref, ka_ref, va_ref, kw_ref, vw_ref, gate_ref, *, tm):
    s0 = pl.program_id(1) * tm
    h = _rms(x_ref[...], g_ref[...]).astype(BF16)
    z = _dot(h, w_ref[...])
    kt = _dot_nt(wkt_ref[...], h)
    o = 0
    xl_ref[...] = z[:, o:o + D_LRU]; o += D_LRU
    gl_ref[...] = z[:, o:o + D_LRU]; o += D_LRU
    zq = z[:, o:o + D_NSA] * (HEAD_DIM ** -0.5); o += D_NSA
    kvc_ref[...] = z[:, o:o + 2 * KV_W]; o += 2 * KV_W
    zvs = z[:, o:o + KV_W]; o += KV_W
    zvw = z[:, o:o + KV_W]; o += KV_W
    gate_ref[...] = jax.nn.sigmoid(z[:, o:o + LANES] + bg_ref[...])

    lane = lax.broadcasted_iota(jnp.int32, (tm, LANES), 1)
    low = lane < HEAD_DIM
    ones_col = jnp.where(lane == HEAD_DIM, 1.0, 0.0)

    def head_block(src, idx):
        blk = src[:, (idx // 2) * LANES:(idx // 2 + 1) * LANES]
        return pltpu.roll(blk, HEAD_DIM, 1) if idx % 2 else blk

    for hh in range(N_HEADS):
        q_ref[hh // GROUP, hh % GROUP] = jnp.where(low, head_block(zq, hh), 0.0).astype(BF16)
    blk_id = lax.broadcasted_iota(jnp.int32, (LANES, tm), 0)
    tok = s0 + lax.broadcasted_iota(jnp.int32, (LANES, tm), 1)
    onehot = jnp.where(tok // SEL_BLOCK == blk_id, 1.0, 0.0).astype(BF16)
    zeros = jnp.zeros((HEAD_DIM, tm), BF16)
    for kv in range(N_KV):
        va_ref[kv] = jnp.where(low, head_block(zvs, kv), ones_col).astype(BF16)
        vw_ref[kv] = jnp.where(low, head_block(zvw, kv), ones_col).astype(BF16)
        ka_ref[kv, 0:HEAD_DIM, :] = kt[kv * HEAD_DIM:(kv + 1) * HEAD_DIM, :].astype(BF16)
        ka_ref[kv, HEAD_DIM:LANES, :] = zeros
        ka_ref[kv, LANES:2 * LANES, :] = onehot
        kw_ref[kv, 0:HEAD_DIM, :] = kt[KV_W + kv * HEAD_DIM:KV_W + (kv + 1) * HEAD_DIM, :].astype(BF16)
        kw_ref[kv, HEAD_DIM:LANES, :] = zeros


def _in_proj(x, g_mix, w_in, b_gate):
    bsz, seq, _ = x.shape
    tm = 512
    o_k = 2 * D_LRU + D_NSA + 2 * KV_W
    cols = lambda a, n: w_in[:, a:a + n]
    w = jnp.concatenate([cols(0, o_k), cols(o_k + KV_W, KV_W), cols(o_k + 3 * KV_W, KV_W),
                         cols(o_k + 4 * KV_W, 3 * N_HEADS)], axis=1)
    w = jnp.pad(w, ((0, 0), (0, LANES - 3 * N_HEADS))).astype(BF16)
    wkt = jnp.concatenate([cols(o_k, KV_W), cols(o_k + 2 * KV_W, KV_W)], axis=1).T.astype(BF16)
    bg = jnp.pad(b_gate, (0, LANES - 3 * N_HEADS)).reshape(1, LANES)
    row = lambda n: pl.BlockSpec((None, tm, n), lambda b, i: (b, i, 0))
    full = lambda a: pl.BlockSpec(a.shape, lambda b, i: (0,) * a.ndim)
    g = g_mix.reshape(1, D_MODEL)
    sds = jax.ShapeDtypeStruct
    out_shape = [sds((bsz, seq, D_LRU), F32), sds((bsz, seq, D_LRU), F32),
                 sds((bsz, N_KV, GROUP, seq, LANES), BF16), sds((bsz, seq, 2 * KV_W), F32),
                 sds((bsz, N_KV, 2 * LANES, seq), BF16), sds((bsz, N_KV, seq, LANES), BF16),
                 sds((bsz, N_KV, LANES, seq), BF16), sds((bsz, N_KV, seq, LANES), BF16),
                 sds((bsz, seq, LANES), F32)]
    keys_t = lambda n: pl.BlockSpec((None, N_KV, n, tm), lambda b, i: (b, 0, 0, i))
    vals = pl.BlockSpec((None, N_KV, tm, LANES), lambda b, i: (b, 0, i, 0))
    out_specs = [row(D_LRU), row(D_LRU),
                 pl.BlockSpec((None, N_KV, GROUP, tm, LANES), lambda b, i: (b, 0, 0, i, 0)),
                 row(2 * KV_W), keys_t(2 * LANES), vals, keys_t(LANES), vals, row(LANES)]
    return pl.pallas_call(
        functools.partial(_inproj_kernel, tm=tm),
        grid=(bsz, seq // tm),
        in_specs=[row(D_MODEL), full(g), full(w), full(wkt), full(bg)],
        out_specs=out_specs,
        out_shape=out_shape,
        compiler_params=_cparams(("parallel", "parallel")),
        name="in_proj",
    )(x, g, w, wkt, bg)


def _rglru_kernel(x_ref, xprev_ref, gate_ref, cw_ref, cb_ref, wa_ref, ba_ref, wi_ref, bi_ref,
                  lam_ref, gout_ref, o_ref, h_ref, *, t):
    i = pl.program_id(1)

    @pl.when(i == 0)
    def _():
        h_ref[...] = jnp.zeros_like(h_ref)

    x = x_ref[...]
    prev = jnp.where(i > 0, xprev_ref[...], 0.0)
    xs = jnp.concatenate([prev, x], axis=0)
    cw = cw_ref[...]
    xc = cb_ref[...] + x * cw[CONV_W - 1:CONV_W, :]
    for k in range(1, CONV_W):
        xk = pltpu.roll(xs, k, 0)[SUBLANES:, :]
        xc = xc + xk * cw[CONV_W - 1 - k:CONV_W - k, :]
    xcb = xc.astype(BF16)
    r = jax.nn.sigmoid(_dot(xcb, wa_ref[...]) + ba_ref[...])
    gi = jax.nn.sigmoid(_dot(xcb, wi_ref[...]) + bi_ref[...])
    lam = lam_ref[...]
    softplus = jnp.maximum(-lam, 0.0) + jnp.log1p(jnp.exp(-jnp.abs(lam)))
    log_a = (-LRU_C) * r * softplus
    a = jnp.exp(log_a)
    b = jnp.sqrt(-jnp.tanh(log_a) * (a * a + 1.0)) * (gi * xc)
    rowi = lax.broadcasted_iota(jnp.int32, (t, D_LRU), 0)
    d = 1
    while d < t:
        ok = rowi >= d
        a_s = pltpu.roll(a, d, 0)
        b_s = pltpu.roll(b, d, 0)
        b = jnp.where(ok, a * b_s, 0.0) + b
        a = jnp.where(ok, a * a_s, a)
        d *= 2
    h = b + a * h_ref[0:1, :]
    h_ref[...] = jnp.broadcast_to(h[t - 1:t, :], h_ref.shape)
    y = h * _gelu(gate_ref[...])
    o_ref[...] = _rms(y, gout_ref[...]).astype(o_ref.dtype)


def _block_diag(w):
    nb, bd, _ = w.shape
    eye = jnp.eye(nb, dtype=w.dtype)
    return (eye[:, None, :, None] * w[:, :, None, :]).reshape(nb * bd, nb * bd)


def _rg_lru(xl, gl, conv_w, conv_b, w_a, b_a, w_i, b_i, lam, g_out, bsz, seq):
    t = 256
    xl3 = xl.reshape(bsz, seq, D_LRU)
    gl3 = gl.reshape(bsz, seq, D_LRU)
    cw = jnp.pad(conv_w, ((0, SUBLANES - CONV_W), (0, 0)))
    vec = lambda a: a.reshape(1, D_LRU)
    wa = _block_diag(w_a).astype(BF16)
    wi = _block_diag(w_i).astype(BF16)
    args = [cw, vec(conv_b), wa, vec(b_a), wi, vec(b_i), vec(lam), vec(g_out)]
    full = lambda a: pl.BlockSpec(a.shape, lambda b, i: (0,) * a.ndim)
    tile = pl.BlockSpec((None, t, D_LRU), lambda b, i: (b, i, 0))
    prev = pl.BlockSpec((None, SUBLANES, D_LRU),
                        lambda b, i: (b, jnp.maximum(i * (t // SUBLANES) - 1, 0), 0))
    return pl.pallas_call(
        functools.partial(_rglru_kernel, t=t),
        grid=(bsz, seq // t),
        in_specs=[tile, prev, tile] + [full(a) for a in args],
        out_specs=tile,
        out_shape=jax.ShapeDtypeStruct((bsz, seq, D_LRU), BF16),
        scratch_shapes=[pltpu.VMEM((SUBLANES, D_LRU), F32)],
        compiler_params=_cparams(("parallel", "arbitrary")),
        name="rg_lru",
    )(xl3, xl3, gl3, *args)


def _compress_kernel(r_ref, pa_ref, pb_ref, w1a_ref, w1b_ref, w2_ref, k_ref, v_ref):
    r = r_ref[...]
    n = r.shape[0]

    def dot3(x, w_ref):
        x_hi, x_mid, _ = _split3(x)
        return _dot(x_hi, w_ref[0]) + _dot(x_hi, w_ref[1]) + _dot(x_mid, w_ref[0])

    first = dot3(r + pa_ref[...], w1a_ref)
    second = dot3(r + pb_ref[...], w1b_ref)
    pre = first + pltpu.roll(second, n - 1, 0)
    out = dot3(_gelu(pre), w2_ref)
    k_ref[...] = out[:, :KV_W]
    v_ref[...] = out[:, KV_W:].astype(v_ref.dtype)


def _compress(kvc, pos_k, w1_k, w2_k, pos_v, w1_v, w2_v, bsz, seq):
    nrow = seq // CMP_STRIDE
    width = CMP_STRIDE * 2 * KV_W
    r = kvc.reshape(bsz, nrow, width)
    eye = jnp.eye(N_KV, dtype=F32)

    def big1(w1, half):
        w = w1.reshape(CMP_BLOCK, HEAD_DIM, HEAD_DIM)[half * CMP_STRIDE:(half + 1) * CMP_STRIDE]
        return jnp.einsum("ldo,hg->lhdgo", w, eye).reshape(CMP_STRIDE, KV_W, KV_W)

    def both1(half):
        z = jnp.zeros((CMP_STRIDE, KV_W, KV_W), F32)
        top = jnp.concatenate([big1(w1_k, half), z], axis=2)
        bot = jnp.concatenate([z, big1(w1_v, half)], axis=2)
        return jnp.concatenate([top, bot], axis=1).reshape(width, 2 * KV_W)

    def pos_row(half):
        pk = jnp.tile(pos_k[half * CMP_STRIDE:(half + 1) * CMP_STRIDE, None, :], (1, N_KV, 1))
        pv = jnp.tile(pos_v[half * CMP_STRIDE:(half + 1) * CMP_STRIDE, None, :], (1, N_KV, 1))
        return jnp.concatenate([pk, pv], axis=1).reshape(1, width)

    def w2big():
        zk = jnp.zeros((KV_W, KV_W), F32)
        k2 = jnp.einsum("do,hg->hdgo", w2_k, eye).reshape(KV_W, KV_W)
        v2 = jnp.einsum("do,hg->hdgo", w2_v, eye).reshape(KV_W, KV_W)
        return jnp.concatenate([jnp.concatenate([k2, zk], 1), jnp.concatenate([zk, v2], 1)], 0)

    pieces = lambda w: jnp.stack(_split3(w)[:2])
    args = [pos_row(0), pos_row(1), pieces(both1(0)), pieces(both1(1)), pieces(w2big())]
    full = lambda a: pl.BlockSpec(a.shape, lambda b: (0,) * a.ndim)
    return pl.pallas_call(
        _compress_kernel,
        grid=(bsz,),
        in_specs=[pl.BlockSpec((None, nrow, width), lambda b: (b, 0, 0))] + [full(a) for a in args],
        out_specs=[pl.BlockSpec((None, nrow, KV_W), lambda b: (b, 0, 0))] * 2,
        out_shape=[jax.ShapeDtypeStruct((bsz, nrow, KV_W), F32),
                   jax.ShapeDtypeStruct((bsz, nrow, KV_W), BF16)],
        compiler_params=_cparams(("parallel",)),
        name="compress",
    )(r, *args)


def _heads_to_lanes(o):
    return jnp.concatenate([o[g * Q_TILE:(g + 1) * Q_TILE, :] for g in range(GROUP)], axis=1)


def _select_kernel(q_ref, kc_ref, vc_ref, ovl_ref, tri_ref, o_ref, bias_ref, *, nsb):
    s0 = pl.program_id(2) * Q_TILE
    rows = GROUP * Q_TILE
    ncp = kc_ref.shape[1]
    q = q_ref[...].reshape(rows, LANES)
    kc_hi, kc_mid, _ = _split3(kc_ref[...])
    s = _dot(q, kc_hi) + _dot(q, kc_mid)
    t = s0 + (lax.broadcasted_iota(jnp.int32, (rows, ncp), 0) & (Q_TILE - 1))
    cend = lax.broadcasted_iota(jnp.int32, (rows, ncp), 1) * CMP_STRIDE + (CMP_BLOCK - 1)
    mask = cend <= t
    sm = jnp.where(mask, s, NEG)
    mx = jnp.max(sm, axis=1, keepdims=True)
    p = jnp.where(mask, jnp.exp(sm - mx), 0.0)
    l = jnp.sum(p, axis=1, keepdims=True)
    pc = p / jnp.maximum(l, 1e-30)
    o = _dot(pc.astype(BF16), vc_ref[...])
    o_ref[...] = _heads_to_lanes(o)
    psum = pc[0:Q_TILE]
    for g in range(1, GROUP):
        psum = psum + pc[g * Q_TILE:(g + 1) * Q_TILE]
    ovl = ovl_ref[...]
    imp = None
    for piece in _split3(psum):
        term = _dot(piece, ovl)
        imp = term if imp is None else imp + term
    tq = s0 + lax.broadcasted_iota(jnp.int32, (Q_TILE, nsb), 0)
    jb = lax.broadcasted_iota(jnp.int32, (Q_TILE, nsb), 1)
    cur = tq // SEL_BLOCK
    forced = (jb == 0) | (jb == cur) | (jb == cur - 1)
    val = jnp.where(jb <= cur, jnp.where(forced, jnp.inf, imp), -jnp.inf)
    vt = val.T
    tau = _topk_rows(vt, want_sorted=False)[0:1, :]
    gt = vt > tau
    tie = vt == tau
    need = float(N_SELECT) - jnp.sum(gt.astype(F32), axis=0, keepdims=True)
    pre = _dot(tri_ref[...], tie.astype(BF16))
    sel = (gt | (tie & (pre <= need))) & (vt > -1.0)
    bias_ref[...] = jnp.where(sel, 0.0, NEG).T.astype(bias_ref.dtype)


def _nsa_select(q4, kc4, vc4, bsz, seq):
    assert seq // SEL_BLOCK <= LANES, "block-selection kernel holds at most 128 selection blocks"
    nsb = LANES
    ncp = kc4.shape[3]
    ci = np.arange(ncp)[:, None] * CMP_STRIDE
    bj = np.arange(nsb)[None, :] * SEL_BLOCK
    ovl = ((ci < bj + SEL_BLOCK) & (ci + CMP_BLOCK > bj) & (np.arange(ncp)[:, None] < ncp - 1))
    ovl = jnp.asarray(ovl, BF16)
    tri = jnp.asarray(np.tril(np.ones((nsb, nsb))), BF16)
    full = lambda a: pl.BlockSpec(a.shape, lambda b, k, i: (0,) * a.ndim)
    return pl.pallas_call(
        functools.partial(_select_kernel, nsb=nsb),
        grid=(bsz, N_KV, seq // Q_TILE),
        in_specs=[pl.BlockSpec((None, None, GROUP, Q_TILE, LANES), lambda b, k, i: (b, k, 0, i, 0)),
                  pl.BlockSpec((None, None, LANES, ncp), lambda b, k, i: (b, k, 0, 0)),
                  pl.BlockSpec((None, None, ncp, HEAD_DIM), lambda b, k, i: (b, k, 0, 0)),
                  full(ovl), full(tri)],
        out_specs=[pl.BlockSpec((None, Q_TILE, GROUP * HEAD_DIM), lambda b, k, i: (b, i, k)),
                   pl.BlockSpec((None, None, Q_TILE, nsb), lambda b, k, i: (b, k, i, 0))],
        out_shape=[jax.ShapeDtypeStruct((bsz, seq, D_NSA), F32),
                   jax.ShapeDtypeStruct((bsz, N_KV, seq, nsb), BF16)],
        compiler_params=_cparams(("parallel", "parallel", "parallel")),
        name="nsa_select",
    )(q4, kc4, vc4, ovl, tri)


def _attend_kernel(q_ref, bias_ref, ka_ref, va_ref, kw_ref, vw_ref, os_ref, ow_ref):
    s0 = pl.program_id(2) * Q_TILE
    rows = GROUP * Q_TILE
    q = q_ref[...].reshape(rows, LANES)
    bias = bias_ref[...]
    qa = jnp.concatenate([q, jnp.concatenate([bias] * GROUP, axis=0)], axis=1)
    t_row = s0 + (lax.broadcasted_iota(jnp.int32, (rows, 1), 0) & (Q_TILE - 1))

    def step(tile, width, carry, masked):
        m, acc = carry
        k0 = pl.multiple_of(tile * width, width)
        s = _dot(qa, ka_ref[:, pl.ds(k0, width)])
        if masked:
            kpos = k0 + lax.broadcasted_iota(jnp.int32, (rows, width), 1)
            s = jnp.where(kpos <= t_row, s, NEG)
        mn = jnp.maximum(m, jnp.max(s, axis=1, keepdims=True))
        p = jnp.exp(s - mn)
        acc = jnp.exp(m - mn) * acc + _dot(p.astype(BF16), va_ref[pl.ds(k0, width), :])
        return mn, acc

    n_diag = s0 // K_DIAG
    n_wide = s0 // K_TILE
    init = (jnp.full((rows, 1), NEG, F32), jnp.zeros((rows, LANES), F32))
    carry = lax.fori_loop(0, n_wide, lambda kt, c: step(kt, K_TILE, c, False), init)
    carry = lax.fori_loop(n_wide * (K_TILE // K_DIAG), n_diag,
                          lambda kt, c: step(kt, K_DIAG, c, False), carry)
    _, acc = step(n_diag, K_DIAG, carry, True)
    os_ref[...] = _heads_to_lanes(acc[:, :HEAD_DIM] / acc[:, HEAD_DIM:HEAD_DIM + 1])

    span = WINDOW + Q_TILE
    w0 = pl.multiple_of(jnp.maximum(s0 - WINDOW, 0), Q_TILE)
    s = _dot(q, kw_ref[:, pl.ds(w0, span)])
    kpos = w0 + lax.broadcasted_iota(jnp.int32, (rows, span), 1)
    mask = (kpos <= t_row) & (kpos > t_row - WINDOW)
    s = jnp.where(mask, s, NEG)
    p = jnp.where(mask, jnp.exp(s - jnp.max(s, axis=1, keepdims=True)), 0.0)
    acc = _dot(p.astype(BF16), vw_ref[pl.ds(w0, span), :])
    ow_ref[...] = _heads_to_lanes(acc[:, :HEAD_DIM] / acc[:, HEAD_DIM:HEAD_DIM + 1])


def _nsa_attend(q4, bias, ka, va, kw, vw, bsz, seq):
    per_head = lambda n: pl.BlockSpec((None, None, seq, n), lambda b, k, i: (b, k, 0, 0))
    per_head_t = lambda n: pl.BlockSpec((None, None, n, seq), lambda b, k, i: (b, k, 0, 0))
    out = pl.BlockSpec((None, Q_TILE, GROUP * HEAD_DIM), lambda b, k, i: (b, i, k))
    return pl.pallas_call(
        _attend_kernel,
        grid=(bsz, N_KV, seq // Q_TILE),
        in_specs=[pl.BlockSpec((None, None, GROUP, Q_TILE, LANES), lambda b, k, i: (b, k, 0, i, 0)),
                  pl.BlockSpec((None, None, Q_TILE, LANES), lambda b, k, i: (b, k, i, 0)),
                  per_head_t(2 * LANES), per_head(LANES), per_head_t(LANES), per_head(LANES)],
        out_specs=[out, out],
        out_shape=[jax.ShapeDtypeStruct((bsz, seq, D_NSA), F32)] * 2,
        compiler_params=_cparams(("parallel", "parallel", "arbitrary")),
        name="nsa_attend",
    )(q4, bias, ka, va, kw, vw)


def _memkv_kernel(m_ref, g_ref, wk_ref, wv_ref, k_ref, v_ref):
    mn = _rms(m_ref[...], g_ref[...]).astype(BF16)
    k_ref[...] = _dot(mn, wk_ref[...]).astype(k_ref.dtype)
    v_ref[...] = _dot(mn, wv_ref[...]).astype(v_ref.dtype)


def _mem_kv(mem, g, wk, wv):
    bsz, nm, _ = mem.shape
    g2 = g.reshape(1, D_MODEL)
    full = lambda a: pl.BlockSpec(a.shape, lambda b: (0,) * a.ndim)
    blk = pl.BlockSpec((None, nm, D_MODEL), lambda b: (b, 0, 0))
    return pl.pallas_call(
        _memkv_kernel,
        grid=(bsz,),
        in_specs=[blk, full(g2), full(wk), full(wv)],
        out_specs=[blk, blk],
        out_shape=[jax.ShapeDtypeStruct(mem.shape, BF16)] * 2,
        compiler_params=_cparams(("parallel",)),
        name="mem_kv",
    )(mem, g2, wk, wv)


def _mix_kernel(x_ref, ylru_ref, oc_ref, os_ref, ow_ref, gate_ref, e_ref, gnsa_ref, wout_ref,
                gq_ref, wq_ref, mk_ref, mv_ref, wo_ref, o_ref):
    g = gate_ref[...]
    g_hi = g.astype(BF16)
    g_lo = (g - g_hi.astype(F32)).astype(BF16)
    e = e_ref[...]
    gx = _dot(g_hi, e) + _dot(g_lo, e)
    y = (gx[:, 0:D_NSA] * oc_ref[...] + gx[:, D_NSA:2 * D_NSA] * os_ref[...]
         + gx[:, 2 * D_NSA:3 * D_NSA] * ow_ref[...])
    yn = _rms(y, gnsa_ref[...]).astype(BF16)
    ycat = jnp.concatenate([ylru_ref[...], yn], axis=1)
    x1 = x_ref[...] + _dot(ycat, wout_ref[...])
    xq = _rms(x1, gq_ref[...]).astype(BF16)
    qm = _dot(xq, wq_ref[...]).astype(BF16)
    outs = []
    for h in range(MEM_HEADS):
        sl = slice(h * MEM_HEAD_DIM, (h + 1) * MEM_HEAD_DIM)
        s = _dot(qm[:, sl], mk_ref[sl, :])
        p = jnp.exp(s - jnp.max(s, axis=1, keepdims=True))
        p = p / jnp.sum(p, axis=1, keepdims=True)
        outs.append(_dot(p.astype(BF16), mv_ref[:, sl]))
    o = jnp.concatenate(outs, axis=1).astype(BF16)
    o_ref[...] = x1 + _dot(o, wo_ref[...])


def _mix_mem(x3, ylru, oc, osl, ow, gates, g_nsa, w_out, g_q, w_mq, mk, mv, w_mo):
    bsz, seq, _ = x3.shape
    tm = 512
    e = np.zeros((LANES, 3 * D_NSA), np.float32)
    for h in range(N_HEADS):
        for j in range(3):
            e[h * 3 + j, j * D_NSA + h * HEAD_DIM:j * D_NSA + (h + 1) * HEAD_DIM] = 1.0
    e = jnp.asarray(e, BF16)
    vec = lambda a: a.reshape(1, -1)
    wq = (w_mq * (MEM_HEAD_DIM ** -0.5)).astype(BF16)
    row = lambda n: pl.BlockSpec((None, tm, n), lambda b, i: (b, i, 0))
    full = lambda a: pl.BlockSpec(a.shape, lambda b, i: (0,) * a.ndim)
    memb = pl.BlockSpec((None, mv.shape[1], D_MODEL), lambda b, i: (b, 0, 0))
    membt = pl.BlockSpec((None, D_MODEL, mv.shape[1]), lambda b, i: (b, 0, 0))
    consts = [e, vec(g_nsa), w_out.astype(BF16), vec(g_q), wq]
    return pl.pallas_call(
        _mix_kernel,
        grid=(bsz, seq // tm),
        in_specs=[row(D_MODEL), row(D_LRU), row(D_NSA), row(D_NSA), row(D_NSA), row(LANES)]
        + [full(a) for a in consts] + [membt, memb, full(w_mo)],
        out_specs=row(D_MODEL),
        out_shape=jax.ShapeDtypeStruct(x3.shape, F32),
        compiler_params=_cparams(("parallel", "parallel")),
        name="mix_mem",
    )(x3, ylru, oc, osl, ow, gates.reshape(bsz, seq, LANES), *consts, mk, mv, w_mo)


def _prefix_count(rows, test):
    pick = jnp.where
    b3 = test(rows[7])
    b2 = test(pick(b3, rows[11], rows[3]))
    b1 = test(pick(b3, pick(b2, rows[13], rows[9]), pick(b2, rows[5], rows[1])))
    b0 = test(pick(b3, pick(b2, pick(b1, rows[14], rows[12]), pick(b1, rows[10], rows[8])),
                   pick(b2, pick(b1, rows[6], rows[4]), pick(b1, rows[2], rows[0]))))
    cnt = pick(b3, 8.0, 0.0) + pick(b2, 4.0, 0.0) + pick(b1, 2.0, 0.0) + pick(b0, 1.0, 0.0)
    return pick(test(rows[15]), 16.0, cnt)


def _peer_pairs():
    k = PEER_TOPK
    return [(a, b) for a in range(k) for b in range(k) if (a + 1) * (b + 1) <= k]


def _route_kernel(x_ref, g_ref, wpq_ref, sk_ref, xn_ref, r2_ref, e2_ref, ns_ref, sc_ref,
                  q_scr, s_scr, l_scr, t_scr, *, tt):
    k = PEER_TOPK
    nhc = 2 * PEER_HEADS
    xn32 = _rms(x_ref[...], g_ref[...])
    xn = xn32.astype(BF16)
    xn_ref[...] = xn32.T.astype(BF16)
    q = _dot(xn, wpq_ref[...]).astype(BF16)
    for hc in range(nhc):
        q_scr[hc] = q[:, hc * PEER_NKEYS:(hc + 1) * PEER_NKEYS]

    def score_and_sort(hc, carry):
        st = _dot_nt(sk_ref[hc], q_scr[hc])
        s_scr[hc] = st
        rows = _topk_rows(st, want_sorted=True)
        h, c = hc // 2, hc % 2
        for a in range(k):
            l_scr[c, a, pl.ds(h, 1), :] = rows[a][0:1, :]
        return carry

    lax.fori_loop(0, nhc, score_and_sort, 0)
    v1 = [l_scr[0, a] for a in range(k)]
    v2 = [l_scr[1, b] for b in range(k)]
    pairs = _peer_pairs()
    cand = [v1[a] + v2[b] for a, b in pairs]
    pad = [jnp.full_like(cand[0], -jnp.inf)] * (4 * k - len(cand))
    groups = [_sort_desc((cand + pad)[i * k:(i + 1) * k]) for i in range(4)]
    left = _bitonic_merge_desc(_top_half(groups[0], groups[1]))
    right = _bitonic_merge_desc(_top_half(groups[2], groups[3]))
    tau = functools.reduce(jnp.minimum, _top_half(left, right))
    top = v1[0] + v2[0]
    z = jnp.zeros_like(top)
    for c_ in cand:
        z = z + jnp.where(c_ >= tau, jnp.exp(c_ - top), 0.0)
    t_scr[0] = tau
    t_scr[1] = 1.0 / z

    def per_head(h, carry):
        s1, s2 = s_scr[2 * h], s_scr[2 * h + 1]
        tau_h = t_scr[0, pl.ds(h, 1), :]
        zinv_h = t_scr[1, pl.ds(h, 1), :]
        l2 = [l_scr[1, b, pl.ds(h, 1), :] for b in range(k)]
        r2_ref[h] = _prefix_count(l2, lambda v: v > s2).astype(r2_ref.dtype)
        ns_ref[h] = _prefix_count(l2, lambda v: (s1 + v) >= tau_h)
        e2_ref[h] = jnp.exp(s2 - l_scr[1, 0, pl.ds(h, 1), :]).astype(e2_ref.dtype)
        sc_ref[h] = jnp.exp(s1 - l_scr[0, 0, pl.ds(h, 1), :]) * zinv_h
        return carry

    lax.fori_loop(0, PEER_HEADS, per_head, 0)


def _peer_route(x2d, g_ffn, w_pq, sub_keys):
    m = x2d.shape[0]
    tt = min(512, m)
    g = g_ffn.reshape(1, D_MODEL)
    wpq = w_pq.astype(BF16)
    sk = sub_keys.astype(BF16).reshape(2 * PEER_HEADS, PEER_NKEYS, PEER_QDIM // 2)
    full = lambda a: pl.BlockSpec(a.shape, lambda i: (0,) * a.ndim)
    tcol = pl.BlockSpec((PEER_HEADS, PEER_NKEYS, tt), lambda i: (0, 0, i))
    shp = lambda dt: jax.ShapeDtypeStruct((PEER_HEADS, PEER_NKEYS, m), dt)
    return pl.pallas_call(
        functools.partial(_route_kernel, tt=tt),
        grid=(m // tt,),
        in_specs=[pl.BlockSpec((tt, D_MODEL), lambda i: (i, 0)), full(g), full(wpq), full(sk)],
        out_specs=[pl.BlockSpec((D_MODEL, tt), lambda i: (0, i)), tcol, tcol, tcol, tcol],
        out_shape=[jax.ShapeDtypeStruct((D_MODEL, m), BF16), shp(BF16), shp(BF16), shp(F32), shp(F32)],
        scratch_shapes=[pltpu.VMEM((2 * PEER_HEADS, tt, PEER_NKEYS), BF16),
                        pltpu.VMEM((2 * PEER_HEADS, PEER_NKEYS, tt), F32),
                        pltpu.VMEM((2, PEER_TOPK, PEER_HEADS, tt), F32),
                        pltpu.VMEM((2, PEER_HEADS, tt), F32)],
        compiler_params=_cparams(("parallel",)),
        name="peer_route",
    )(x2d, g, wpq, sk)


def _dense_kernel(x_ref, xn_ref, r2_ref, e2_ref, ns_ref, sc_ref, u_ref, vt_ref, gf_ref,
                  o_ref, acc_ref, a_ref, *, i1b, tt):
    ei = pl.program_id(1)

    @pl.when(ei == 0)
    def _():
        acc_ref[...] = jnp.zeros_like(acc_ref)

    pack = 2 * SUBLANES
    tiles = PEER_NKEYS // pack
    view = lambda a: a.reshape(tiles, pack, tt)

    def row_tile(ref, h, i1):
        return jnp.broadcast_to(ref[h, pl.ds(i1, 1), :], (pack, tt)).astype(BF16)[None]

    gh = _gelu(_dot(u_ref[...], xn_ref[...]).astype(BF16))
    for kk in range(i1b):
        i1 = ei * i1b + kk
        w = None
        for h in range(PEER_HEADS):
            n_tile = row_tile(ns_ref, h, i1)
            s_tile = row_tile(sc_ref, h, i1)
            term = jnp.where(view(r2_ref[h]) < n_tile, view(e2_ref[h]), jnp.zeros((), BF16)) * s_tile
            w = term if w is None else w + term
        rs = slice(kk * PEER_NKEYS, (kk + 1) * PEER_NKEYS)
        a_ref[rs, :] = w.reshape(PEER_NKEYS, tt) * gh[rs, :]
    acc_ref[...] += _dot(vt_ref[...], a_ref[...])

    @pl.when(ei == pl.num_programs(1) - 1)
    def _():
        o_ref[...] = _rms(x_ref[...] + acc_ref[...].T, gf_ref[...])


def _peer_dense(x2d, xn, r2, e2, ns, sc, peer_u, peer_v, g_final):
    m = x2d.shape[0]
    tt = min(512, m)
    i1b = 8
    te = i1b * PEER_NKEYS
    u = peer_u.astype(BF16)
    vt = peer_v.T.astype(BF16)
    gf = g_final.reshape(1, D_MODEL)
    tok = pl.BlockSpec((tt, D_MODEL), lambda t, e: (t, 0))
    tcol = pl.BlockSpec((PEER_HEADS, PEER_NKEYS, tt), lambda t, e: (0, 0, t))
    return pl.pallas_call(
        functools.partial(_dense_kernel, i1b=i1b, tt=tt),
        grid=(m // tt, PEER_NKEYS // i1b),
        in_specs=[tok, pl.BlockSpec((D_MODEL, tt), lambda t, e: (0, t)), tcol, tcol, tcol, tcol,
                  pl.BlockSpec((te, D_MODEL), lambda t, e: (e, 0)),
                  pl.BlockSpec((D_MODEL, te), lambda t, e: (0, e)),
                  pl.BlockSpec((1, D_MODEL), lambda t, e: (0, 0))],
        out_specs=tok,
        out_shape=jax.ShapeDtypeStruct((m, D_MODEL), F32),
        scratch_shapes=[pltpu.VMEM((D_MODEL, tt), F32), pltpu.VMEM((te, tt), BF16)],
        compiler_params=_cparams(("parallel", "arbitrary")),
        name="peer_dense",
    )(x2d, xn, r2, e2, ns, sc, u, vt, gf)


def _per_kv_head(a, bsz, seq):
    return a.reshape(bsz, seq, N_KV, HEAD_DIM).transpose(0, 2, 1, 3)


def _layer(x, mem, p):
    bsz, seq, _ = x.shape
    m = bsz * seq
    xl, gl, q4, kvc, ka, va, kw, vw, gates = _in_proj(x, p["g_mix"], p["w_in"], p["b_gate"])
    ylru = _rg_lru(xl, gl, p["conv_w"], p["conv_b"], p["w_rg_a"], p["b_rg_a"], p["w_rg_i"],
                   p["b_rg_i"], p["lam"], p["g_out_lru"], bsz, seq)
    kc, vc = _compress(kvc, p["cmp_pos_k"], p["cmp_w1_k"], p["cmp_w2_k"], p["cmp_pos_v"],
                       p["cmp_w1_v"], p["cmp_w2_v"], bsz, seq)
    ncp = seq // CMP_STRIDE
    pad64 = lambda a: jnp.pad(a, ((0, 0),) * (a.ndim - 1) + ((0, LANES - HEAD_DIM),))
    swap = lambda a: jnp.swapaxes(a, -1, -2)
    kc4 = swap(pad64(kc.reshape(bsz, ncp, N_KV, HEAD_DIM).transpose(0, 2, 1, 3)))
    vc4 = vc.reshape(bsz, ncp, N_KV, HEAD_DIM).transpose(0, 2, 1, 3)
    o_cmp, bias = _nsa_select(q4, kc4, vc4, bsz, seq)
    o_slc, o_win = _nsa_attend(q4, bias, ka, va, kw, vw, bsz, seq)

    mk, mv = _mem_kv(mem, p["g_mem_kv"], p["w_mk"].astype(BF16), p["w_mv"].astype(BF16))
    x2 = _mix_mem(x, ylru, o_cmp, o_slc, o_win, gates, p["g_out_nsa"], p["w_out"], p["g_mem_q"],
                  p["w_mq"], swap(mk), mv, p["w_mo"].astype(BF16))
    return x2.reshape(m, D_MODEL)


def kernel(x, mem, g_mix, w_in, b_gate, conv_w, conv_b, w_rg_a, b_rg_a, w_rg_i, b_rg_i, lam,
           cmp_pos_k, cmp_w1_k, cmp_w2_k, cmp_pos_v, cmp_w1_v, cmp_w2_v, g_out_lru, g_out_nsa,
           w_out, g_mem_q, g_mem_kv, w_mq, w_mk, w_mv, w_mo, g_ffn, w_pq, sub_keys, peer_u,
           peer_v, g_final):
    depth = w_in.shape[0]
    assert depth == 1, "the final norm is fused into the single layer's PEER kernel"
    names = ["g_mix", "w_in", "b_gate", "conv_w", "conv_b", "w_rg_a", "b_rg_a", "w_rg_i", "b_rg_i",
             "lam", "cmp_pos_k", "cmp_w1_k", "cmp_w2_k", "cmp_pos_v", "cmp_w1_v", "cmp_w2_v",
             "g_out_lru", "g_out_nsa", "w_out", "g_mem_q", "g_mem_kv", "w_mq", "w_mk", "w_mv", "w_mo",
             "g_ffn", "w_pq", "sub_keys", "peer_u", "peer_v"]
    vals = [g_mix, w_in, b_gate, conv_w, conv_b, w_rg_a, b_rg_a, w_rg_i, b_rg_i, lam, cmp_pos_k,
            cmp_w1_k, cmp_w2_k, cmp_pos_v, cmp_w1_v, cmp_w2_v, g_out_lru, g_out_nsa, w_out, g_mem_q,
            g_mem_kv, w_mq, w_mk, w_mv, w_mo, g_ffn, w_pq, sub_keys, peer_u, peer_v]
    p = {n: v[0] for n, v in zip(names, vals)}
    bsz, seq, _ = x.shape
    x2 = _layer(x, mem, p)
    xn, r2, e2, ns, sc = _peer_route(x2, p["g_ffn"], p["w_pq"], p["sub_keys"])
    out = _peer_dense(x2, xn, r2, e2, ns, sc, p["peer_u"], p["peer_v"], g_final)
    return out.reshape(bsz, seq, D_MODEL)
```

```python
import functools

import numpy as np
import jax
import jax.numpy as jnp
from jax import lax
from jax.experimental import pallas as pl
from jax.experimental.pallas import tpu as pltpu

F32 = jnp.float32
BF16 = jnp.bfloat16

D_MODEL = 1024
N_MEM = 256
D_LRU = 512
LRU_BLOCKS = 8
CONV_W = 4
LRU_C = 8.0
N_HEADS = 8
N_KV = 2
GROUP = N_HEADS // N_KV
HEAD_DIM = 64
D_NSA = N_HEADS * HEAD_DIM
CMP_BLOCK = 32
CMP_STRIDE = 16
SEL_BLOCK = 64
N_SELECT = 16
WINDOW = 512
MEM_HEADS = 4
MEM_HEAD_DIM = D_MODEL // MEM_HEADS
PEER_HEADS = 8
PEER_NKEYS = 128
PEER_TOPK = 16
PEER_QDIM = 256
EPS = 1e-6
KV_W = N_KV * HEAD_DIM
N_IN = 2 * D_LRU + D_NSA + 6 * KV_W + 3 * N_HEADS
N_IN_PAD = 2 * D_LRU + D_NSA + 6 * KV_W + 128

LANES = 128
SUBLANES = 8
NEG = -1e30
VMEM_LIMIT = 56 * 1024 * 1024

Q_TILE = 256
K_TILE = 1024
K_DIAG = 512


def _cparams(sem):
    return pltpu.CompilerParams(dimension_semantics=sem, vmem_limit_bytes=VMEM_LIMIT)


def _rms(x, g):
    return x * lax.rsqrt(jnp.mean(x * x, axis=-1, keepdims=True) + EPS) * g


def _gelu(x):
    c = 0.7978845608028654
    return 0.5 * x * (1.0 + jnp.tanh(c * (x + 0.044715 * (x * x * x))))


def _dot(a, b):
    return jnp.dot(a, b, preferred_element_type=F32)


def _dot_nt(a, b):
    return lax.dot_general(a, b, (((1,), (1,)), ((), ())), preferred_element_type=F32)


def _split3(a):
    hi = a.astype(BF16)
    r1 = a - hi.astype(F32)
    mid = r1.astype(BF16)
    lo = (r1 - mid.astype(F32)).astype(BF16)
    return hi, mid, lo


def _batcher_pairs(n):
    pairs = []
    p = 1
    while p < n:
        k = p
        while k >= 1:
            for j in range(k % p, n - k, 2 * k):
                for i in range(min(k, n - j - k)):
                    if (i + j) // (2 * p) == (i + j + k) // (2 * p):
                        pairs.append((i + j, i + j + k))
            k //= 2
        p *= 2
    return pairs


_SORT16 = _batcher_pairs(16)


def _sort_desc(xs):
    xs = list(xs)
    for i, j in _batcher_pairs(len(xs)):
        a, b = xs[i], xs[j]
        xs[i], xs[j] = jnp.maximum(a, b), jnp.minimum(a, b)
    return xs


def _bitonic_merge_desc(xs):
    xs = list(xs)
    n = len(xs)
    stride = n // 2
    while stride >= 1:
        for i in range(n):
            if (i & stride) == 0:
                a, b = xs[i], xs[i + stride]
                xs[i], xs[i + stride] = jnp.maximum(a, b), jnp.minimum(a, b)
        stride //= 2
    return xs


def _top_half(a, b):
    n = len(a)
    return [jnp.maximum(a[i], b[n - 1 - i]) for i in range(n)]


def _topk_rows(x, *, want_sorted):
    k = PEER_TOPK
    rows = [x[SUBLANES * v:SUBLANES * (v + 1), :] for v in range(k)]
    rows = _sort_desc(rows)
    for step, shift in enumerate((4, 2, 1)):
        part = [pltpu.roll(r, shift, 0) for r in rows]
        rows = _top_half(rows, part)
        if step < 2 or want_sorted:
            rows = _bitonic_merge_desc(rows)
    if want_sorted:
        return rows
    return functools.reduce(jnp.minimum, rows)


def _inproj_kernel(x_ref, g_ref, w_ref, wkt_ref, bg_ref,
                   xl_ref, gl_ref, q_ref, kvc_ref, ka_ref, va_ref, kw_ref, vw_ref, gate_ref, *, tm):
    s0 = pl.program_id(1) * tm
    h = _rms(x_ref[...], g_ref[...]).astype(BF16)
    z = _dot(h, w_ref[...])
    kt = _dot_nt(wkt_ref[...], h)
    o = 0
    xl_ref[...] = z[:, o:o + D_LRU]; o += D_LRU
    gl_ref[...] = z[:, o:o + D_LRU]; o += D_LRU
    zq = z[:, o:o + D_NSA] * (HEAD_DIM ** -0.5); o += D_NSA
    kvc_ref[...] = z[:, o:o + 2 * KV_W]; o += 2 * KV_W
    zvs = z[:, o:o + KV_W]; o += KV_W
    zvw = z[:, o:o + KV_W]; o += KV_W
    gate_ref[...] = jax.nn.sigmoid(z[:, o:o + LANES] + bg_ref[...])

    lane = lax.broadcasted_iota(jnp.int32, (tm, LANES), 1)
    low = lane < HEAD_DIM
    ones_col = jnp.where(lane == HEAD_DIM, 1.0, 0.0)

    def head_block(src, idx):
        blk = src[:, (idx // 2) * LANES:(idx // 2 + 1) * LANES]
        return pltpu.roll(blk, HEAD_DIM, 1) if idx % 2 else blk

    for hh in range(N_HEADS):
        q_ref[hh // GROUP, hh % GROUP] = jnp.where(low, head_block(zq, hh), 0.0).astype(BF16)
    blk_id = lax.broadcasted_iota(jnp.int32, (LANES, tm), 0)
    tok = s0 + lax.broadcasted_iota(jnp.int32, (LANES, tm), 1)
    onehot = jnp.where(tok // SEL_BLOCK == blk_id, 1.0, 0.0).astype(BF16)
    zeros = jnp.zeros((HEAD_DIM, tm), BF16)
    for kv in range(N_KV):
        va_ref[kv] = jnp.where(low, head_block(zvs, kv), ones_col).astype(BF16)
        vw_ref[kv] = jnp.where(low, head_block(zvw, kv), ones_col).astype(BF16)
        ka_ref[kv, 0:HEAD_DIM, :] = kt[kv * HEAD_DIM:(kv + 1) * HEAD_DIM, :].astype(BF16)
        ka_ref[kv, HEAD_DIM:LANES, :] = zeros
        ka_ref[kv, LANES:2 * LANES, :] = onehot
        kw_ref[kv, 0:HEAD_DIM, :] = kt[KV_W + kv * HEAD_DIM:KV_W + (kv + 1) * HEAD_DIM, :].astype(BF16)
        kw_ref[kv, HEAD_DIM:LANES, :] = zeros


def _in_proj(x, g_mix, w_in, b_gate):
    bsz, seq, _ = x.shape
    tm = 512
    o_k = 2 * D_LRU + D_NSA + 2 * KV_W
    cols = lambda a, n: w_in[:, a:a + n]
    w = jnp.concatenate([cols(0, o_k), cols(o_k + KV_W, KV_W), cols(o_k + 3 * KV_W, KV_W),
                         cols(o_k + 4 * KV_W, 3 * N_HEADS)], axis=1)
    w = jnp.pad(w, ((0, 0), (0, LANES - 3 * N_HEADS))).astype(BF16)
    wkt = jnp.concatenate([cols(o_k, KV_W), cols(o_k + 2 * KV_W, KV_W)], axis=1).T.astype(BF16)
    bg = jnp.pad(b_gate, (0, LANES - 3 * N_HEADS)).reshape(1, LANES)
    row = lambda n: pl.BlockSpec((None, tm, n), lambda b, i: (b, i, 0))
    full = lambda a: pl.BlockSpec(a.shape, lambda b, i: (0,) * a.ndim)
    g = g_mix.reshape(1, D_MODEL)
    sds = jax.ShapeDtypeStruct
    out_shape = [sds((bsz, seq, D_LRU), F32), sds((bsz, seq, D_LRU), F32),
                 sds((bsz, N_KV, GROUP, seq, LANES), BF16), sds((bsz, seq, 2 * KV_W), F32),
                 sds((bsz, N_KV, 2 * LANES, seq), BF16), sds((bsz, N_KV, seq, LANES), BF16),
                 sds((bsz, N_KV, LANES, seq), BF16), sds((bsz, N_KV, seq, LANES), BF16),
                 sds((bsz, seq, LANES), F32)]
    keys_t = lambda n: pl.BlockSpec((None, N_KV, n, tm), lambda b, i: (b, 0, 0, i))
    vals = pl.BlockSpec((None, N_KV, tm, LANES), lambda b, i: (b, 0, i, 0))
    out_specs = [row(D_LRU), row(D_LRU),
                 pl.BlockSpec((None, N_KV, GROUP, tm, LANES), lambda b, i: (b, 0, 0, i, 0)),
                 row(2 * KV_W), keys_t(2 * LANES), vals, keys_t(LANES), vals, row(LANES)]
    return pl.pallas_call(
        functools.partial(_inproj_kernel, tm=tm),
        grid=(bsz, seq // tm),
        in_specs=[row(D_MODEL), full(g), full(w), full(wkt), full(bg)],
        out_specs=out_specs,
        out_shape=out_shape,
        compiler_params=_cparams(("parallel", "parallel")),
        name="in_proj",
    )(x, g, w, wkt, bg)


def _rglru_kernel(x_ref, xprev_ref, gate_ref, cw_ref, cb_ref, wa_ref, ba_ref, wi_ref, bi_ref,
                  lam_ref, gout_ref, o_ref, h_ref, *, t):
    i = pl.program_id(1)

    @pl.when(i == 0)
    def _():
        h_ref[...] = jnp.zeros_like(h_ref)

    x = x_ref[...]
    prev = jnp.where(i > 0, xprev_ref[...], 0.0)
    xs = jnp.concatenate([prev, x], axis=0)
    cw = cw_ref[...]
    xc = cb_ref[...] + x * cw[CONV_W - 1:CONV_W, :]
    for k in range(1, CONV_W):
        xk = pltpu.roll(xs, k, 0)[SUBLANES:, :]
        xc = xc + xk * cw[CONV_W - 1 - k:CONV_W - k, :]
    xcb = xc.astype(BF16)
    r = jax.nn.sigmoid(_dot(xcb, wa_ref[...]) + ba_ref[...])
    gi = jax.nn.sigmoid(_dot(xcb, wi_ref[...]) + bi_ref[...])
    lam = lam_ref[...]
    softplus = jnp.maximum(-lam, 0.0) + jnp.log1p(jnp.exp(-jnp.abs(lam)))
    log_a = (-LRU_C) * r * softplus
    a = jnp.exp(log_a)
    b = jnp.sqrt(-jnp.tanh(log_a) * (a * a + 1.0)) * (gi * xc)
    rowi = lax.broadcasted_iota(jnp.int32, (t, D_LRU), 0)
    d = 1
    while d < t:
        ok = rowi >= d
        a_s = pltpu.roll(a, d, 0)
        b_s = pltpu.roll(b, d, 0)
        b = jnp.where(ok, a * b_s, 0.0) + b
        a = jnp.where(ok, a * a_s, a)
        d *= 2
    h = b + a * h_ref[0:1, :]
    h_ref[...] = jnp.broadcast_to(h[t - 1:t, :], h_ref.shape)
    y = h * _gelu(gate_ref[...])
    o_ref[...] = _rms(y, gout_ref[...]).astype(o_ref.dtype)


def _block_diag(w):
    nb, bd, _ = w.shape
    eye = jnp.eye(nb, dtype=w.dtype)
    return (eye[:, None, :, None] * w[:, :, None, :]).reshape(nb * bd, nb * bd)


def _rg_lru(xl, gl, conv_w, conv_b, w_a, b_a, w_i, b_i, lam, g_out, bsz, seq):
    t = 256
    xl3 = xl.reshape(bsz, seq, D_LRU)
    gl3 = gl.reshape(bsz, seq, D_LRU)
    cw = jnp.pad(conv_w, ((0, SUBLANES - CONV_W), (0, 0)))
    vec = lambda a: a.reshape(1, D_LRU)
    wa = _block_diag(w_a).astype(BF16)
    wi = _block_diag(w_i).astype(BF16)
    args = [cw, vec(conv_b), wa, vec(b_a), wi, vec(b_i), vec(lam), vec(g_out)]
    full = lambda a: pl.BlockSpec(a.shape, lambda b, i: (0,) * a.ndim)
    tile = pl.BlockSpec((None, t, D_LRU), lambda b, i: (b, i, 0))
    prev = pl.BlockSpec((None, SUBLANES, D_LRU),
                        lambda b, i: (b, jnp.maximum(i * (t // SUBLANES) - 1, 0), 0))
    return pl.pallas_call(
        functools.partial(_rglru_kernel, t=t),
        grid=(bsz, seq // t),
        in_specs=[tile, prev, tile] + [full(a) for a in args],
        out_specs=tile,
        out_shape=jax.ShapeDtypeStruct((bsz, seq, D_LRU), BF16),
        scratch_shapes=[pltpu.VMEM((SUBLANES, D_LRU), F32)],
        compiler_params=_cparams(("parallel", "arbitrary")),
        name="rg_lru",
    )(xl3, xl3, gl3, *args)


def _compress_kernel(r_ref, pa_ref, pb_ref, w1a_ref, w1b_ref, w2_ref, k_ref, v_ref):
    r = r_ref[...]
    n = r.shape[0]

    def dot3(x, w_ref):
        x_hi, x_mid, _ = _split3(x)
        return _dot(x_hi, w_ref[0]) + _dot(x_hi, w_ref[1]) + _dot(x_mid, w_ref[0])

    first = dot3(r + pa_ref[...], w1a_ref)
    second = dot3(r + pb_ref[...], w1b_ref)
    pre = first + pltpu.roll(second, n - 1, 0)
    out = dot3(_gelu(pre), w2_ref)
    k_ref[...] = out[:, :KV_W]
    v_ref[...] = out[:, KV_W:].astype(v_ref.dtype)


def _compress(kvc, pos_k, w1_k, w2_k, pos_v, w1_v, w2_v, bsz, seq):
    nrow = seq // CMP_STRIDE
    width = CMP_STRIDE * 2 * KV_W
    r = kvc.reshape(bsz, nrow, width)
    eye = jnp.eye(N_KV, dtype=F32)

    def big1(w1, half):
        w = w1.reshape(CMP_BLOCK, HEAD_DIM, HEAD_DIM)[half * CMP_STRIDE:(half + 1) * CMP_STRIDE]
        return jnp.einsum("ldo,hg->lhdgo", w, eye).reshape(CMP_STRIDE, KV_W, KV_W)

    def both1(half):
        z = jnp.zeros((CMP_STRIDE, KV_W, KV_W), F32)
        top = jnp.concatenate([big1(w1_k, half), z], axis=2)
        bot = jnp.concatenate([z, big1(w1_v, half)], axis=2)
        return jnp.concatenate([top, bot], axis=1).reshape(width, 2 * KV_W)

    def pos_row(half):
        pk = jnp.tile(pos_k[half * CMP_STRIDE:(half + 1) * CMP_STRIDE, None, :], (1, N_KV, 1))
        pv = jnp.tile(pos_v[half * CMP_STRIDE:(half + 1) * CMP_STRIDE, None, :], (1, N_KV, 1))
        return jnp.concatenate([pk, pv], axis=1).reshape(1, width)

    def w2big():
        zk = jnp.zeros((KV_W, KV_W), F32)
        k2 = jnp.einsum("do,hg->hdgo", w2_k, eye).reshape(KV_W, KV_W)
        v2 = jnp.einsum("do,hg->hdgo", w2_v, eye).reshape(KV_W, KV_W)
        return jnp.concatenate([jnp.concatenate([k2, zk], 1), jnp.concatenate([zk, v2], 1)], 0)

    pieces = lambda w: jnp.stack(_split3(w)[:2])
    args = [pos_row(0), pos_row(1), pieces(both1(0)), pieces(both1(1)), pieces(w2big())]
    full = lambda a: pl.BlockSpec(a.shape, lambda b: (0,) * a.ndim)
    return pl.pallas_call(
        _compress_kernel,
        grid=(bsz,),
        in_specs=[pl.BlockSpec((None, nrow, width), lambda b: (b, 0, 0))] + [full(a) for a in args],
        out_specs=[pl.BlockSpec((None, nrow, KV_W), lambda b: (b, 0, 0))] * 2,
        out_shape=[jax.ShapeDtypeStruct((bsz, nrow, KV_W), F32),
                   jax.ShapeDtypeStruct((bsz, nrow, KV_W), BF16)],
        compiler_params=_cparams(("parallel",)),
        name="compress",
    )(r, *args)


def _heads_to_lanes(o):
    return jnp.concatenate([o[g * Q_TILE:(g + 1) * Q_TILE, :] for g in range(GROUP)], axis=1)


def _select_kernel(q_ref, kc_ref, vc_ref, ovl_ref, tri_ref, o_ref, bias_ref, *, nsb):
    s0 = pl.program_id(2) * Q_TILE
    rows = GROUP * Q_TILE
    ncp = kc_ref.shape[1]
    q = q_ref[...].reshape(rows, LANES)
    kc_hi, kc_mid, _ = _split3(kc_ref[...])
    s = _dot(q, kc_hi) + _dot(q, kc_mid)
    t = s0 + (lax.broadcasted_iota(jnp.int32, (rows, ncp), 0) & (Q_TILE - 1))
    cend = lax.broadcasted_iota(jnp.int32, (rows, ncp), 1) * CMP_STRIDE + (CMP_BLOCK - 1)
    mask = cend <= t
    sm = jnp.where(mask, s, NEG)
    mx = jnp.max(sm, axis=1, keepdims=True)
    p = jnp.where(mask, jnp.exp(sm - mx), 0.0)
    l = jnp.sum(p, axis=1, keepdims=True)
    pc = p / jnp.maximum(l, 1e-30)
    o = _dot(pc.astype(BF16), vc_ref[...])
    o_ref[...] = _heads_to_lanes(o)
    psum = pc[0:Q_TILE]
    for g in range(1, GROUP):
        psum = psum + pc[g * Q_TILE:(g + 1) * Q_TILE]
    ovl = ovl_ref[...]
    imp = None
    for piece in _split3(psum):
        term = _dot(piece, ovl)
        imp = term if imp is None else imp + term
    tq = s0 + lax.broadcasted_iota(jnp.int32, (Q_TILE, nsb), 0)
    jb = lax.broadcasted_iota(jnp.int32, (Q_TILE, nsb), 1)
    cur = tq // SEL_BLOCK
    forced = (jb == 0) | (jb == cur) | (jb == cur - 1)
    val = jnp.where(jb <= cur, jnp.where(forced, jnp.inf, imp), -jnp.inf)
    vt = val.T
    tau = _topk_rows(vt, want_sorted=False)[0:1, :]
    gt = vt > tau
    tie = vt == tau
    need = float(N_SELECT) - jnp.sum(gt.astype(F32), axis=0, keepdims=True)
    pre = _dot(tri_ref[...], tie.astype(BF16))
    sel = (gt | (tie & (pre <= need))) & (vt > -1.0)
    bias_ref[...] = jnp.where(sel, 0.0, NEG).T.astype(bias_ref.dtype)


def _nsa_select(q4, kc4, vc4, bsz, seq):
    assert seq // SEL_BLOCK <= LANES, "block-selection kernel holds at most 128 selection blocks"
    nsb = LANES
    ncp = kc4.shape[3]
    ci = np.arange(ncp)[:, None] * CMP_STRIDE
    bj = np.arange(nsb)[None, :] * SEL_BLOCK
    ovl = ((ci < bj + SEL_BLOCK) & (ci + CMP_BLOCK > bj) & (np.arange(ncp)[:, None] < ncp - 1))
    ovl = jnp.asarray(ovl, BF16)
    tri = jnp.asarray(np.tril(np.ones((nsb, nsb))), BF16)
    full = lambda a: pl.BlockSpec(a.shape, lambda b, k, i: (0,) * a.ndim)
    return pl.pallas_call(
        functools.partial(_select_kernel, nsb=nsb),
        grid=(bsz, N_KV, seq // Q_TILE),
        in_specs=[pl.BlockSpec((None, None, GROUP, Q_TILE, LANES), lambda b, k, i: (b, k, 0, i, 0)),
                  pl.BlockSpec((None, None, LANES, ncp), lambda b, k, i: (b, k, 0, 0)),
                  pl.BlockSpec((None, None, ncp, HEAD_DIM), lambda b, k, i: (b, k, 0, 0)),
                  full(ovl), full(tri)],
        out_specs=[pl.BlockSpec((None, Q_TILE, GROUP * HEAD_DIM), lambda b, k, i: (b, i, k)),
                   pl.BlockSpec((None, None, Q_TILE, nsb), lambda b, k, i: (b, k, i, 0))],
        out_shape=[jax.ShapeDtypeStruct((bsz, seq, D_NSA), F32),
                   jax.ShapeDtypeStruct((bsz, N_KV, seq, nsb), BF16)],
        compiler_params=_cparams(("parallel", "parallel", "parallel")),
        name="nsa_select",
    )(q4, kc4, vc4, ovl, tri)


def _attend_kernel(q_ref, bias_ref, ka_ref, va_ref, kw_ref, vw_ref, os_ref, ow_ref):
    s0 = pl.program_id(2) * Q_TILE
    rows = GROUP * Q_TILE
    q = q_ref[...].reshape(rows, LANES)
    bias = bias_ref[...]
    qa = jnp.concatenate([q, jnp.concatenate([bias] * GROUP, axis=0)], axis=1)
    t_row = s0 + (lax.broadcasted_iota(jnp.int32, (rows, 1), 0) & (Q_TILE - 1))

    def step(tile, width, carry, masked):
        m, acc = carry
        k0 = pl.multiple_of(tile * width, width)
        s = _dot(qa, ka_ref[:, pl.ds(k0, width)])
        if masked:
            kpos = k0 + lax.broadcasted_iota(jnp.int32, (rows, width), 1)
            s = jnp.where(kpos <= t_row, s, NEG)
        mn = jnp.maximum(m, jnp.max(s, axis=1, keepdims=True))
        p = jnp.exp(s - mn)
        acc = jnp.exp(m - mn) * acc + _dot(p.astype(BF16), va_ref[pl.ds(k0, width), :])
        return mn, acc

    n_diag = s0 // K_DIAG
    n_wide = s0 // K_TILE
    init = (jnp.full((rows, 1), NEG, F32), jnp.zeros((rows, LANES), F32))
    carry = lax.fori_loop(0, n_wide, lambda kt, c: step(kt, K_TILE, c, False), init)
    carry = lax.fori_loop(n_wide * (K_TILE // K_DIAG), n_diag,
                          lambda kt, c: step(kt, K_DIAG, c, False), carry)
    _, acc = step(n_diag, K_DIAG, carry, True)
    os_ref[...] = _heads_to_lanes(acc[:, :HEAD_DIM] / acc[:, HEAD_DIM:HEAD_DIM + 1])

    span = WINDOW + Q_TILE
    w0 = pl.multiple_of(jnp.maximum(s0 - WINDOW, 0), Q_TILE)
    s = _dot(q, kw_ref[:, pl.ds(w0, span)])
    kpos = w0 + lax.broadcasted_iota(jnp.int32, (rows, span), 1)
    mask = (kpos <= t_row) & (kpos > t_row - WINDOW)
    s = jnp.where(mask, s, NEG)
    p = jnp.where(mask, jnp.exp(s - jnp.max(s, axis=1, keepdims=True)), 0.0)
    acc = _dot(p.astype(BF16), vw_ref[pl.ds(w0, span), :])
    ow_ref[...] = _heads_to_lanes(acc[:, :HEAD_DIM] / acc[:, HEAD_DIM:HEAD_DIM + 1])


def _nsa_attend(q4, bias, ka, va, kw, vw, bsz, seq):
    per_head = lambda n: pl.BlockSpec((None, None, seq, n), lambda b, k, i: (b, k, 0, 0))
    per_head_t = lambda n: pl.BlockSpec((None, None, n, seq), lambda b, k, i: (b, k, 0, 0))
    out = pl.BlockSpec((None, Q_TILE, GROUP * HEAD_DIM), lambda b, k, i: (b, i, k))
    return pl.pallas_call(
        _attend_kernel,
        grid=(bsz, N_KV, seq // Q_TILE),
        in_specs=[pl.BlockSpec((None, None, GROUP, Q_TILE, LANES), lambda b, k, i: (b, k, 0, i, 0)),
                  pl.BlockSpec((None, None, Q_TILE, LANES), lambda b, k, i: (b, k, i, 0)),
                  per_head_t(2 * LANES), per_head(LANES), per_head_t(LANES), per_head(LANES)],
        out_specs=[out, out],
        out_shape=[jax.ShapeDtypeStruct((bsz, seq, D_NSA), F32)] * 2,
        compiler_params=_cparams(("parallel", "parallel", "arbitrary")),
        name="nsa_attend",
    )(q4, bias, ka, va, kw, vw)


def _memkv_kernel(m_ref, g_ref, wk_ref, wv_ref, k_ref, v_ref):
    mn = _rms(m_ref[...], g_ref[...]).astype(BF16)
    k_ref[...] = _dot(mn, wk_ref[...]).astype(k_ref.dtype)
    v_ref[...] = _dot(mn, wv_ref[...]).astype(v_ref.dtype)


def _mem_kv(mem, g, wk, wv):
    bsz, nm, _ = mem.shape
    g2 = g.reshape(1, D_MODEL)
    full = lambda a: pl.BlockSpec(a.shape, lambda b: (0,) * a.ndim)
    blk = pl.BlockSpec((None, nm, D_MODEL), lambda b: (b, 0, 0))
    return pl.pallas_call(
        _memkv_kernel,
        grid=(bsz,),
        in_specs=[blk, full(g2), full(wk), full(wv)],
        out_specs=[blk, blk],
        out_shape=[jax.ShapeDtypeStruct(mem.shape, BF16)] * 2,
        compiler_params=_cparams(("parallel",)),
        name="mem_kv",
    )(mem, g2, wk, wv)


def _mix_kernel(x_ref, ylru_ref, oc_ref, os_ref, ow_ref, gate_ref, e_ref, gnsa_ref, wout_ref,
                gq_ref, wq_ref, mk_ref, mv_ref, wo_ref, o_ref):
    g = gate_ref[...]
    g_hi = g.astype(BF16)
    g_lo = (g - g_hi.astype(F32)).astype(BF16)
    e = e_ref[...]
    gx = _dot(g_hi, e) + _dot(g_lo, e)
    y = (gx[:, 0:D_NSA] * oc_ref[...] + gx[:, D_NSA:2 * D_NSA] * os_ref[...]
         + gx[:, 2 * D_NSA:3 * D_NSA] * ow_ref[...])
    yn = _rms(y, gnsa_ref[...]).astype(BF16)
    ycat = jnp.concatenate([ylru_ref[...], yn], axis=1)
    x1 = x_ref[...] + _dot(ycat, wout_ref[...])
    xq = _rms(x1, gq_ref[...]).astype(BF16)
    qm = _dot(xq, wq_ref[...]).astype(BF16)
    outs = []
    for h in range(MEM_HEADS):
        sl = slice(h * MEM_HEAD_DIM, (h + 1) * MEM_HEAD_DIM)
        s = _dot(qm[:, sl], mk_ref[sl, :])
        p = jnp.exp(s - jnp.max(s, axis=1, keepdims=True))
        p = p / jnp.sum(p, axis=1, keepdims=True)
        outs.append(_dot(p.astype(BF16), mv_ref[:, sl]))
    o = jnp.concatenate(outs, axis=1).astype(BF16)
    o_ref[...] = x1 + _dot(o, wo_ref[...])


def _mix_mem(x3, ylru, oc, osl, ow, gates, g_nsa, w_out, g_q, w_mq, mk, mv, w_mo):
    bsz, seq, _ = x3.shape
    tm = 512
    e = np.zeros((LANES, 3 * D_NSA), np.float32)
    for h in range(N_HEADS):
        for j in range(3):
            e[h * 3 + j, j * D_NSA + h * HEAD_DIM:j * D_NSA + (h + 1) * HEAD_DIM] = 1.0
    e = jnp.asarray(e, BF16)
    vec = lambda a: a.reshape(1, -1)
    wq = (w_mq * (MEM_HEAD_DIM ** -0.5)).astype(BF16)
    row = lambda n: pl.BlockSpec((None, tm, n), lambda b, i: (b, i, 0))
    full = lambda a: pl.BlockSpec(a.shape, lambda b, i: (0,) * a.ndim)
    memb = pl.BlockSpec((None, mv.shape[1], D_MODEL), lambda b, i: (b, 0, 0))
    membt = pl.BlockSpec((None, D_MODEL, mv.shape[1]), lambda b, i: (b, 0, 0))
    consts = [e, vec(g_nsa), w_out.astype(BF16), vec(g_q), wq]
    return pl.pallas_call(
        _mix_kernel,
        grid=(bsz, seq // tm),
        in_specs=[row(D_MODEL), row(D_LRU), row(D_NSA), row(D_NSA), row(D_NSA), row(LANES)]
        + [full(a) for a in consts] + [membt, memb, full(w_mo)],
        out_specs=row(D_MODEL),
        out_shape=jax.ShapeDtypeStruct(x3.shape, F32),
        compiler_params=_cparams(("parallel", "parallel")),
        name="mix_mem",
    )(x3, ylru, oc, osl, ow, gates.reshape(bsz, seq, LANES), *consts, mk, mv, w_mo)


def _prefix_count(rows, test):
    pick = jnp.where
    b3 = test(rows[7])
    b2 = test(pick(b3, rows[11], rows[3]))
    b1 = test(pick(b3, pick(b2, rows[13], rows[9]), pick(b2, rows[5], rows[1])))
    b0 = test(pick(b3, pick(b2, pick(b1, rows[14], rows[12]), pick(b1, rows[10], rows[8])),
                   pick(b2, pick(b1, rows[6], rows[4]), pick(b1, rows[2], rows[0]))))
    cnt = pick(b3, 8.0, 0.0) + pick(b2, 4.0, 0.0) + pick(b1, 2.0, 0.0) + pick(b0, 1.0, 0.0)
    return pick(test(rows[15]), 16.0, cnt)


def _peer_pairs():
    k = PEER_TOPK
    return [(a, b) for a in range(k) for b in range(k) if (a + 1) * (b + 1) <= k]


def _route_kernel(x_ref, g_ref, wpq_ref, sk_ref, xn_ref, r2_ref, e2_ref, ns_ref, sc_ref,
                  q_scr, s_scr, l_scr, t_scr, *, tt):
    k = PEER_TOPK
    nhc = 2 * PEER_HEADS
    xn32 = _rms(x_ref[...], g_ref[...])
    xn = xn32.astype(BF16)
    xn_ref[...] = xn32.T.astype(BF16)
    q = _dot(xn, wpq_ref[...]).astype(BF16)
    for hc in range(nhc):
        q_scr[hc] = q[:, hc * PEER_NKEYS:(hc + 1) * PEER_NKEYS]

    def score_and_sort(hc, carry):
        st = _dot_nt(sk_ref[hc], q_scr[hc])
        s_scr[hc] = st
        rows = _topk_rows(st, want_sorted=True)
        h, c = hc // 2, hc % 2
        for a in range(k):
            l_scr[c, a, pl.ds(h, 1), :] = rows[a][0:1, :]
        return carry

    lax.fori_loop(0, nhc, score_and_sort, 0)
    v1 = [l_scr[0, a] for a in range(k)]
    v2 = [l_scr[1, b] for b in range(k)]
    pairs = _peer_pairs()
    cand = [v1[a] + v2[b] for a, b in pairs]
    pad = [jnp.full_like(cand[0], -jnp.inf)] * (4 * k - len(cand))
    groups = [_sort_desc((cand + pad)[i * k:(i + 1) * k]) for i in range(4)]
    left = _bitonic_merge_desc(_top_half(groups[0], groups[1]))
    right = _bitonic_merge_desc(_top_half(groups[2], groups[3]))
    tau = functools.reduce(jnp.minimum, _top_half(left, right))
    top = v1[0] + v2[0]
    z = jnp.zeros_like(top)
    for c_ in cand:
        z = z + jnp.where(c_ >= tau, jnp.exp(c_ - top), 0.0)
    t_scr[0] = tau
    t_scr[1] = 1.0 / z

    def per_head(h, carry):
        s1, s2 = s_scr[2 * h], s_scr[2 * h + 1]
        tau_h = t_scr[0, pl.ds(h, 1), :]
        zinv_h = t_scr[1, pl.ds(h, 1), :]
        l2 = [l_scr[1, b, pl.ds(h, 1), :] for b in range(k)]
        r2_ref[h] = _prefix_count(l2, lambda v: v > s2).astype(r2_ref.dtype)
        ns_ref[h] = _prefix_count(l2, lambda v: (s1 + v) >= tau_h)
        e2_ref[h] = jnp.exp(s2 - l_scr[1, 0, pl.ds(h, 1), :]).astype(e2_ref.dtype)
        sc_ref[h] = jnp.exp(s1 - l_scr[0, 0, pl.ds(h, 1), :]) * zinv_h
        return carry

    lax.fori_loop(0, PEER_HEADS, per_head, 0)


def _peer_route(x2d, g_ffn, w_pq, sub_keys):
    m = x2d.shape[0]
    tt = min(512, m)
    g = g_ffn.reshape(1, D_MODEL)
    wpq = w_pq.astype(BF16)
    sk = sub_keys.astype(BF16).reshape(2 * PEER_HEADS, PEER_NKEYS, PEER_QDIM // 2)
    full = lambda a: pl.BlockSpec(a.shape, lambda i: (0,) * a.ndim)
    tcol = pl.BlockSpec((PEER_HEADS, PEER_NKEYS, tt), lambda i: (0, 0, i))
    shp = lambda dt: jax.ShapeDtypeStruct((PEER_HEADS, PEER_NKEYS, m), dt)
    return pl.pallas_call(
        functools.partial(_route_kernel, tt=tt),
        grid=(m // tt,),
        in_specs=[pl.BlockSpec((tt, D_MODEL), lambda i: (i, 0)), full(g), full(wpq), full(sk)],
        out_specs=[pl.BlockSpec((D_MODEL, tt), lambda i: (0, i)), tcol, tcol, tcol, tcol],
        out_shape=[jax.ShapeDtypeStruct((D_MODEL, m), BF16), shp(BF16), shp(BF16), shp(F32), shp(F32)],
        scratch_shapes=[pltpu.VMEM((2 * PEER_HEADS, tt, PEER_NKEYS), BF16),
                        pltpu.VMEM((2 * PEER_HEADS, PEER_NKEYS, tt), F32),
                        pltpu.VMEM((2, PEER_TOPK, PEER_HEADS, tt), F32),
                        pltpu.VMEM((2, PEER_HEADS, tt), F32)],
        compiler_params=_cparams(("parallel",)),
        name="peer_route",
    )(x2d, g, wpq, sk)


def _dense_kernel(x_ref, xn_ref, r2_ref, e2_ref, ns_ref, sc_ref, u_ref, vt_ref, gf_ref,
                  o_ref, acc_ref, a_ref, *, i1b, tt):
    ei = pl.program_id(1)

    @pl.when(ei == 0)
    def _():
        acc_ref[...] = jnp.zeros_like(acc_ref)

    pack = 2 * SUBLANES
    tiles = PEER_NKEYS // pack
    view = lambda a: a.reshape(tiles, pack, tt)

    def row_tile(ref, h, i1):
        return jnp.broadcast_to(ref[h, pl.ds(i1, 1), :], (pack, tt)).astype(BF16)[None]

    gh = _gelu(_dot(u_ref[...], xn_ref[...]).astype(BF16))
    for kk in range(i1b):
        i1 = ei * i1b + kk
        w = None
        for h in range(PEER_HEADS):
            n_tile = row_tile(ns_ref, h, i1)
            s_tile = row_tile(sc_ref, h, i1)
            term = jnp.where(view(r2_ref[h]) < n_tile, view(e2_ref[h]), jnp.zeros((), BF16)) * s_tile
            w = term if w is None else w + term
        rs = slice(kk * PEER_NKEYS, (kk + 1) * PEER_NKEYS)
        a_ref[rs, :] = w.reshape(PEER_NKEYS, tt) * gh[rs, :]
    acc_ref[...] += _dot(vt_ref[...], a_ref[...])

    @pl.when(ei == pl.num_programs(1) - 1)
    def _():
        o_ref[...] = _rms(x_ref[...] + acc_ref[...].T, gf_ref[...])


def _peer_dense(x2d, xn, r2, e2, ns, sc, peer_u, peer_v, g_final):
    m = x2d.shape[0]
    tt = min(512, m)
    i1b = 16
    te = i1b * PEER_NKEYS
    u = peer_u.astype(BF16)
    vt = peer_v.T.astype(BF16)
    gf = g_final.reshape(1, D_MODEL)
    tok = pl.BlockSpec((tt, D_MODEL), lambda t, e: (t, 0))
    tcol = pl.BlockSpec((PEER_HEADS, PEER_NKEYS, tt), lambda t, e: (0, 0, t))
    return pl.pallas_call(
        functools.partial(_dense_kernel, i1b=i1b, tt=tt),
        grid=(m // tt, PEER_NKEYS // i1b),
        in_specs=[tok, pl.BlockSpec((D_MODEL, tt), lambda t, e: (0, t)), tcol, tcol, tcol, tcol,
                  pl.BlockSpec((te, D_MODEL), lambda t, e: (e, 0)),
                  pl.BlockSpec((D_MODEL, te), lambda t, e: (0, e)),
                  pl.BlockSpec((1, D_MODEL), lambda t, e: (0, 0))],
        out_specs=tok,
        out_shape=jax.ShapeDtypeStruct((m, D_MODEL), F32),
        scratch_shapes=[pltpu.VMEM((D_MODEL, tt), F32), pltpu.VMEM((te, tt), BF16)],
        compiler_params=_cparams(("parallel", "arbitrary")),
        name="peer_dense",
    )(x2d, xn, r2, e2, ns, sc, u, vt, gf)


def _per_kv_head(a, bsz, seq):
    return a.reshape(bsz, seq, N_KV, HEAD_DIM).transpose(0, 2, 1, 3)


def _layer(x, mem, p):
    bsz, seq, _ = x.shape
    m = bsz * seq
    xl, gl, q4, kvc, ka, va, kw, vw, gates = _in_proj(x, p["g_mix"], p["w_in"], p["b_gate"])
    ylru = _rg_lru(xl, gl, p["conv_w"], p["conv_b"], p["w_rg_a"], p["b_rg_a"], p["w_rg_i"],
                   p["b_rg_i"], p["lam"], p["g_out_lru"], bsz, seq)
    kc, vc = _compress(kvc, p["cmp_pos_k"], p["cmp_w1_k"], p["cmp_w2_k"], p["cmp_pos_v"],
                       p["cmp_w1_v"], p["cmp_w2_v"], bsz, seq)
    ncp = seq // CMP_STRIDE
    pad64 = lambda a: jnp.pad(a, ((0, 0),) * (a.ndim - 1) + ((0, LANES - HEAD_DIM),))
    swap = lambda a: jnp.swapaxes(a, -1, -2)
    kc4 = swap(pad64(kc.reshape(bsz, ncp, N_KV, HEAD_DIM).transpose(0, 2, 1, 3)))
    vc4 = vc.reshape(bsz, ncp, N_KV, HEAD_DIM).transpose(0, 2, 1, 3)
    o_cmp, bias = _nsa_select(q4, kc4, vc4, bsz, seq)
    o_slc, o_win = _nsa_attend(q4, bias, ka, va, kw, vw, bsz, seq)

    mk, mv = _mem_kv(mem, p["g_mem_kv"], p["w_mk"].astype(BF16), p["w_mv"].astype(BF16))
    x2 = _mix_mem(x, ylru, o_cmp, o_slc, o_win, gates, p["g_out_nsa"], p["w_out"], p["g_mem_q"],
                  p["w_mq"], swap(mk), mv, p["w_mo"].astype(BF16))
    return x2.reshape(m, D_MODEL)


def kernel(x, mem, g_mix, w_in, b_gate, conv_w, conv_b, w_rg_a, b_rg_a, w_rg_i, b_rg_i, lam,
           cmp_pos_k, cmp_w1_k, cmp_w2_k, cmp_pos_v, cmp_w1_v, cmp_w2_v, g_out_lru, g_out_nsa,
           w_out, g_mem_q, g_mem_kv, w_mq, w_mk, w_mv, w_mo, g_ffn, w_pq, sub_keys, peer_u,
           peer_v, g_final):
    depth = w_in.shape[0]
    assert depth == 1, "the final norm is fused into the single layer's PEER kernel"
    names = ["g_mix", "w_in", "b_gate", "conv_w", "conv_b", "w_rg_a", "b_rg_a", "w_rg_i", "b_rg_i",
             "lam", "cmp_pos_k", "cmp_w1_k", "cmp_w2_k", "cmp_pos_v", "cmp_w1_v", "cmp_w2_v",
             "g_out_lru", "g_out_nsa", "w_out", "g_mem_q", "g_mem_kv", "w_mq", "w_mk", "w_mv", "w_mo",
             "g_ffn", "w_pq", "sub_keys", "peer_u", "peer_v"]
    vals = [g_mix, w_in, b_gate, conv_w, conv_b, w_rg_a, b_rg_a, w_rg_i, b_rg_i, lam, cmp_pos_k,
            cmp_w1_k, cmp_w2_k, cmp_pos_v, cmp_w1_v, cmp_w2_v, g_out_lru, g_out_nsa, w_out, g_mem_q,
            g_mem_kv, w_mq, w_mk, w_mv, w_mo, g_ffn, w_pq, sub_keys, peer_u, peer_v]
    p = {n: v[0] for n, v in zip(names, vals)}
    bsz, seq, _ = x.shape
    x2 = _layer(x, mem, p)
    xn, r2, e2, ns, sc = _peer_route(x2, p["g_ffn"], p["w_pq"], p["sub_keys"])
    out = _peer_dense(x2, xn, r2, e2, ns, sc, p["peer_u"], p["peer_v"], g_final)
    return out.reshape(bsz, seq, D_MODEL)
```

```python
import functools

import numpy as np
import jax
import jax.numpy as jnp
from jax import lax
from jax.experimental import pallas as pl
from jax.experimental.pallas import tpu as pltpu

F32 = jnp.float32
BF16 = jnp.bfloat16

D_MODEL = 1024
N_MEM = 256
D_LRU = 512
LRU_BLOCKS = 8
CONV_W = 4
LRU_C = 8.0
N_HEADS = 8
N_KV = 2
GROUP = N_HEADS // N_KV
HEAD_DIM = 64
D_NSA = N_HEADS * HEAD_DIM
CMP_BLOCK = 32
CMP_STRIDE = 16
SEL_BLOCK = 64
N_SELECT = 16
WINDOW = 512
MEM_HEADS = 4
MEM_HEAD_DIM = D_MODEL // MEM_HEADS
PEER_HEADS = 8
PEER_NKEYS = 128
PEER_TOPK = 16
PEER_QDIM = 256
EPS = 1e-6
KV_W = N_KV * HEAD_DIM
N_IN = 2 * D_LRU + D_NSA + 6 * KV_W + 3 * N_HEADS
N_IN_PAD = 2 * D_LRU + D_NSA + 6 * KV_W + 128

LANES = 128
SUBLANES = 8
NEG = -1e30
VMEM_LIMIT = 56 * 1024 * 1024

Q_TILE = 256
K_TILE = 2048
K_DIAG = 512


def _cparams(sem):
    return pltpu.CompilerParams(dimension_semantics=sem, vmem_limit_bytes=VMEM_LIMIT)


def _rms(x, g):
    return x * lax.rsqrt(jnp.mean(x * x, axis=-1, keepdims=True) + EPS) * g


def _gelu(x):
    c = 0.7978845608028654
    return 0.5 * x * (1.0 + jnp.tanh(c * (x + 0.044715 * (x * x * x))))


def _dot(a, b):
    return jnp.dot(a, b, preferred_element_type=F32)


def _dot_nt(a, b):
    return lax.dot_general(a, b, (((1,), (1,)), ((), ())), preferred_element_type=F32)


def _split3(a):
    hi = a.astype(BF16)
    r1 = a - hi.astype(F32)
    mid = r1.astype(BF16)
    lo = (r1 - mid.astype(F32)).astype(BF16)
    return hi, mid, lo


def _batcher_pairs(n):
    pairs = []
    p = 1
    while p < n:
        k = p
        while k >= 1:
            for j in range(k % p, n - k, 2 * k):
                for i in range(min(k, n - j - k)):
                    if (i + j) // (2 * p) == (i + j + k) // (2 * p):
                        pairs.append((i + j, i + j + k))
            k //= 2
        p *= 2
    return pairs


_SORT16 = _batcher_pairs(16)


def _sort_desc(xs):
    xs = list(xs)
    for i, j in _batcher_pairs(len(xs)):
        a, b = xs[i], xs[j]
        xs[i], xs[j] = jnp.maximum(a, b), jnp.minimum(a, b)
    return xs


def _bitonic_merge_desc(xs):
    xs = list(xs)
    n = len(xs)
    stride = n // 2
    while stride >= 1:
        for i in range(n):
            if (i & stride) == 0:
                a, b = xs[i], xs[i + stride]
                xs[i], xs[i + stride] = jnp.maximum(a, b), jnp.minimum(a, b)
        stride //= 2
    return xs


def _top_half(a, b):
    n = len(a)
    return [jnp.maximum(a[i], b[n - 1 - i]) for i in range(n)]


def _topk_rows(x, *, want_sorted):
    k = PEER_TOPK
    rows = [x[SUBLANES * v:SUBLANES * (v + 1), :] for v in range(k)]
    rows = _sort_desc(rows)
    for step, shift in enumerate((4, 2, 1)):
        part = [pltpu.roll(r, shift, 0) for r in rows]
        rows = _top_half(rows, part)
        if step < 2 or want_sorted:
            rows = _bitonic_merge_desc(rows)
    if want_sorted:
        return rows
    return functools.reduce(jnp.minimum, rows)


def _inproj_kernel(x_ref, g_ref, w_ref, wkt_ref, bg_ref,
                   xl_ref, gl_ref, q_ref, kvc_ref, ka_ref, va_ref, kw_ref, vw_ref, gate_ref, *, tm):
    s0 = pl.program_id(1) * tm
    h = _rms(x_ref[...], g_ref[...]).astype(BF16)
    z = _dot(h, w_ref[...])
    kt = _dot_nt(wkt_ref[...], h)
    o = 0
    xl_ref[...] = z[:, o:o + D_LRU]; o += D_LRU
    gl_ref[...] = z[:, o:o + D_LRU]; o += D_LRU
    zq = z[:, o:o + D_NSA] * (HEAD_DIM ** -0.5); o += D_NSA
    kvc_ref[...] = z[:, o:o + 2 * KV_W]; o += 2 * KV_W
    zvs = z[:, o:o + KV_W]; o += KV_W
    zvw = z[:, o:o + KV_W]; o += KV_W
    gate_ref[...] = jax.nn.sigmoid(z[:, o:o + LANES] + bg_ref[...])

    lane = lax.broadcasted_iota(jnp.int32, (tm, LANES), 1)
    low = lane < HEAD_DIM
    ones_col = jnp.where(lane == HEAD_DIM, 1.0, 0.0)

    def head_block(src, idx):
        blk = src[:, (idx // 2) * LANES:(idx // 2 + 1) * LANES]
        return pltpu.roll(blk, HEAD_DIM, 1) if idx % 2 else blk

    for hh in range(N_HEADS):
        q_ref[hh // GROUP, hh % GROUP] = jnp.where(low, head_block(zq, hh), 0.0).astype(BF16)
    blk_id = lax.broadcasted_iota(jnp.int32, (LANES, tm), 0)
    tok = s0 + lax.broadcasted_iota(jnp.int32, (LANES, tm), 1)
    onehot = jnp.where(tok // SEL_BLOCK == blk_id, 1.0, 0.0).astype(BF16)
    zeros = jnp.zeros((HEAD_DIM, tm), BF16)
    for kv in range(N_KV):
        va_ref[kv] = jnp.where(low, head_block(zvs, kv), ones_col).astype(BF16)
        vw_ref[kv] = jnp.where(low, head_block(zvw, kv), ones_col).astype(BF16)
        ka_ref[kv, 0:HEAD_DIM, :] = kt[kv * HEAD_DIM:(kv + 1) * HEAD_DIM, :].astype(BF16)
        ka_ref[kv, HEAD_DIM:LANES, :] = zeros
        ka_ref[kv, LANES:2 * LANES, :] = onehot
        kw_ref[kv, 0:HEAD_DIM, :] = kt[KV_W + kv * HEAD_DIM:KV_W + (kv + 1) * HEAD_DIM, :].astype(BF16)
        kw_ref[kv, HEAD_DIM:LANES, :] = zeros


def _in_proj(x, g_mix, w_in, b_gate):
    bsz, seq, _ = x.shape
    tm = 512
    o_k = 2 * D_LRU + D_NSA + 2 * KV_W
    cols = lambda a, n: w_in[:, a:a + n]
    w = jnp.concatenate([cols(0, o_k), cols(o_k + KV_W, KV_W), cols(o_k + 3 * KV_W, KV_W),
                         cols(o_k + 4 * KV_W, 3 * N_HEADS)], axis=1)
    w = jnp.pad(w, ((0, 0), (0, LANES - 3 * N_HEADS))).astype(BF16)
    wkt = jnp.concatenate([cols(o_k, KV_W), cols(o_k + 2 * KV_W, KV_W)], axis=1).T.astype(BF16)
    bg = jnp.pad(b_gate, (0, LANES - 3 * N_HEADS)).reshape(1, LANES)
    row = lambda n: pl.BlockSpec((None, tm, n), lambda b, i: (b, i, 0))
    full = lambda a: pl.BlockSpec(a.shape, lambda b, i: (0,) * a.ndim)
    g = g_mix.reshape(1, D_MODEL)
    sds = jax.ShapeDtypeStruct
    out_shape = [sds((bsz, seq, D_LRU), F32), sds((bsz, seq, D_LRU), F32),
                 sds((bsz, N_KV, GROUP, seq, LANES), BF16), sds((bsz, seq, 2 * KV_W), F32),
                 sds((bsz, N_KV, 2 * LANES, seq), BF16), sds((bsz, N_KV, seq, LANES), BF16),
                 sds((bsz, N_KV, LANES, seq), BF16), sds((bsz, N_KV, seq, LANES), BF16),
                 sds((bsz, seq, LANES), F32)]
    keys_t = lambda n: pl.BlockSpec((None, N_KV, n, tm), lambda b, i: (b, 0, 0, i))
    vals = pl.BlockSpec((None, N_KV, tm, LANES), lambda b, i: (b, 0, i, 0))
    out_specs = [row(D_LRU), row(D_LRU),
                 pl.BlockSpec((None, N_KV, GROUP, tm, LANES), lambda b, i: (b, 0, 0, i, 0)),
                 row(2 * KV_W), keys_t(2 * LANES), vals, keys_t(LANES), vals, row(LANES)]
    return pl.pallas_call(
        functools.partial(_inproj_kernel, tm=tm),
        grid=(bsz, seq // tm),
        in_specs=[row(D_MODEL), full(g), full(w), full(wkt), full(bg)],
        out_specs=out_specs,
        out_shape=out_shape,
        compiler_params=_cparams(("parallel", "parallel")),
        name="in_proj",
    )(x, g, w, wkt, bg)


def _rglru_kernel(x_ref, xprev_ref, gate_ref, cw_ref, cb_ref, wa_ref, ba_ref, wi_ref, bi_ref,
                  lam_ref, gout_ref, o_ref, h_ref, *, t):
    i = pl.program_id(1)

    @pl.when(i == 0)
    def _():
        h_ref[...] = jnp.zeros_like(h_ref)

    x = x_ref[...]
    prev = jnp.where(i > 0, xprev_ref[...], 0.0)
    xs = jnp.concatenate([prev, x], axis=0)
    cw = cw_ref[...]
    xc = cb_ref[...] + x * cw[CONV_W - 1:CONV_W, :]
    for k in range(1, CONV_W):
        xk = pltpu.roll(xs, k, 0)[SUBLANES:, :]
        xc = xc + xk * cw[CONV_W - 1 - k:CONV_W - k, :]
    xcb = xc.astype(BF16)
    r = jax.nn.sigmoid(_dot(xcb, wa_ref[...]) + ba_ref[...])
    gi = jax.nn.sigmoid(_dot(xcb, wi_ref[...]) + bi_ref[...])
    lam = lam_ref[...]
    softplus = jnp.maximum(-lam, 0.0) + jnp.log1p(jnp.exp(-jnp.abs(lam)))
    log_a = (-LRU_C) * r * softplus
    a = jnp.exp(log_a)
    b = jnp.sqrt(-jnp.tanh(log_a) * (a * a + 1.0)) * (gi * xc)
    rowi = lax.broadcasted_iota(jnp.int32, (t, D_LRU), 0)
    d = 1
    while d < t:
        ok = rowi >= d
        a_s = pltpu.roll(a, d, 0)
        b_s = pltpu.roll(b, d, 0)
        b = jnp.where(ok, a * b_s, 0.0) + b
        a = jnp.where(ok, a * a_s, a)
        d *= 2
    h = b + a * h_ref[0:1, :]
    h_ref[...] = jnp.broadcast_to(h[t - 1:t, :], h_ref.shape)
    y = h * _gelu(gate_ref[...])
    o_ref[...] = _rms(y, gout_ref[...]).astype(o_ref.dtype)


def _block_diag(w):
    nb, bd, _ = w.shape
    eye = jnp.eye(nb, dtype=w.dtype)
    return (eye[:, None, :, None] * w[:, :, None, :]).reshape(nb * bd, nb * bd)


def _rg_lru(xl, gl, conv_w, conv_b, w_a, b_a, w_i, b_i, lam, g_out, bsz, seq):
    t = 256
    xl3 = xl.reshape(bsz, seq, D_LRU)
    gl3 = gl.reshape(bsz, seq, D_LRU)
    cw = jnp.pad(conv_w, ((0, SUBLANES - CONV_W), (0, 0)))
    vec = lambda a: a.reshape(1, D_LRU)
    wa = _block_diag(w_a).astype(BF16)
    wi = _block_diag(w_i).astype(BF16)
    args = [cw, vec(conv_b), wa, vec(b_a), wi, vec(b_i), vec(lam), vec(g_out)]
    full = lambda a: pl.BlockSpec(a.shape, lambda b, i: (0,) * a.ndim)
    tile = pl.BlockSpec((None, t, D_LRU), lambda b, i: (b, i, 0))
    prev = pl.BlockSpec((None, SUBLANES, D_LRU),
                        lambda b, i: (b, jnp.maximum(i * (t // SUBLANES) - 1, 0), 0))
    return pl.pallas_call(
        functools.partial(_rglru_kernel, t=t),
        grid=(bsz, seq // t),
        in_specs=[tile, prev, tile] + [full(a) for a in args],
        out_specs=tile,
        out_shape=jax.ShapeDtypeStruct((bsz, seq, D_LRU), BF16),
        scratch_shapes=[pltpu.VMEM((SUBLANES, D_LRU), F32)],
        compiler_params=_cparams(("parallel", "arbitrary")),
        name="rg_lru",
    )(xl3, xl3, gl3, *args)


def _compress_kernel(r_ref, pa_ref, pb_ref, w1a_ref, w1b_ref, w2_ref, k_ref, v_ref):
    r = r_ref[...]
    n = r.shape[0]

    def dot3(x, w_ref):
        x_hi, x_mid, _ = _split3(x)
        return _dot(x_hi, w_ref[0]) + _dot(x_hi, w_ref[1]) + _dot(x_mid, w_ref[0])

    first = dot3(r + pa_ref[...], w1a_ref)
    second = dot3(r + pb_ref[...], w1b_ref)
    pre = first + pltpu.roll(second, n - 1, 0)
    out = dot3(_gelu(pre), w2_ref)
    k_ref[...] = out[:, :KV_W]
    v_ref[...] = out[:, KV_W:].astype(v_ref.dtype)


def _compress(kvc, pos_k, w1_k, w2_k, pos_v, w1_v, w2_v, bsz, seq):
    nrow = seq // CMP_STRIDE
    width = CMP_STRIDE * 2 * KV_W
    r = kvc.reshape(bsz, nrow, width)
    eye = jnp.eye(N_KV, dtype=F32)

    def big1(w1, half):
        w = w1.reshape(CMP_BLOCK, HEAD_DIM, HEAD_DIM)[half * CMP_STRIDE:(half + 1) * CMP_STRIDE]
        return jnp.einsum("ldo,hg->lhdgo", w, eye).reshape(CMP_STRIDE, KV_W, KV_W)

    def both1(half):
        z = jnp.zeros((CMP_STRIDE, KV_W, KV_W), F32)
        top = jnp.concatenate([big1(w1_k, half), z], axis=2)
        bot = jnp.concatenate([z, big1(w1_v, half)], axis=2)
        return jnp.concatenate([top, bot], axis=1).reshape(width, 2 * KV_W)

    def pos_row(half):
        pk = jnp.tile(pos_k[half * CMP_STRIDE:(half + 1) * CMP_STRIDE, None, :], (1, N_KV, 1))
        pv = jnp.tile(pos_v[half * CMP_STRIDE:(half + 1) * CMP_STRIDE, None, :], (1, N_KV, 1))
        return jnp.concatenate([pk, pv], axis=1).reshape(1, width)

    def w2big():
        zk = jnp.zeros((KV_W, KV_W), F32)
        k2 = jnp.einsum("do,hg->hdgo", w2_k, eye).reshape(KV_W, KV_W)
        v2 = jnp.einsum("do,hg->hdgo", w2_v, eye).reshape(KV_W, KV_W)
        return jnp.concatenate([jnp.concatenate([k2, zk], 1), jnp.concatenate([zk, v2], 1)], 0)

    pieces = lambda w: jnp.stack(_split3(w)[:2])
    args = [pos_row(0), pos_row(1), pieces(both1(0)), pieces(both1(1)), pieces(w2big())]
    full = lambda a: pl.BlockSpec(a.shape, lambda b: (0,) * a.ndim)
    return pl.pallas_call(
        _compress_kernel,
        grid=(bsz,),
        in_specs=[pl.BlockSpec((None, nrow, width), lambda b: (b, 0, 0))] + [full(a) for a in args],
        out_specs=[pl.BlockSpec((None, nrow, KV_W), lambda b: (b, 0, 0))] * 2,
        out_shape=[jax.ShapeDtypeStruct((bsz, nrow, KV_W), F32),
                   jax.ShapeDtypeStruct((bsz, nrow, KV_W), BF16)],
        compiler_params=_cparams(("parallel",)),
        name="compress",
    )(r, *args)


def _heads_to_lanes(o):
    return jnp.concatenate([o[g * Q_TILE:(g + 1) * Q_TILE, :] for g in range(GROUP)], axis=1)


def _select_kernel(q_ref, kc_ref, vc_ref, ovl_ref, tri_ref, o_ref, bias_ref, *, nsb):
    s0 = pl.program_id(2) * Q_TILE
    rows = GROUP * Q_TILE
    ncp = kc_ref.shape[1]
    q = q_ref[...].reshape(rows, LANES)
    kc_hi, kc_mid, _ = _split3(kc_ref[...])
    s = _dot(q, kc_hi) + _dot(q, kc_mid)
    t = s0 + (lax.broadcasted_iota(jnp.int32, (rows, ncp), 0) & (Q_TILE - 1))
    cend = lax.broadcasted_iota(jnp.int32, (rows, ncp), 1) * CMP_STRIDE + (CMP_BLOCK - 1)
    mask = cend <= t
    sm = jnp.where(mask, s, NEG)
    mx = jnp.max(sm, axis=1, keepdims=True)
    p = jnp.where(mask, jnp.exp(sm - mx), 0.0)
    l = jnp.sum(p, axis=1, keepdims=True)
    pc = p / jnp.maximum(l, 1e-30)
    o = _dot(pc.astype(BF16), vc_ref[...])
    o_ref[...] = _heads_to_lanes(o)
    psum = pc[0:Q_TILE]
    for g in range(1, GROUP):
        psum = psum + pc[g * Q_TILE:(g + 1) * Q_TILE]
    ovl = ovl_ref[...]
    imp = None
    for piece in _split3(psum):
        term = _dot(piece, ovl)
        imp = term if imp is None else imp + term
    tq = s0 + lax.broadcasted_iota(jnp.int32, (Q_TILE, nsb), 0)
    jb = lax.broadcasted_iota(jnp.int32, (Q_TILE, nsb), 1)
    cur = tq // SEL_BLOCK
    forced = (jb == 0) | (jb == cur) | (jb == cur - 1)
    val = jnp.where(jb <= cur, jnp.where(forced, jnp.inf, imp), -jnp.inf)
    vt = val.T
    tau = _topk_rows(vt, want_sorted=False)[0:1, :]
    gt = vt > tau
    tie = vt == tau
    need = float(N_SELECT) - jnp.sum(gt.astype(F32), axis=0, keepdims=True)
    pre = _dot(tri_ref[...], tie.astype(BF16))
    sel = (gt | (tie & (pre <= need))) & (vt > -1.0)
    bias_ref[...] = jnp.where(sel, 0.0, NEG).T.astype(bias_ref.dtype)


def _nsa_select(q4, kc4, vc4, bsz, seq):
    assert seq // SEL_BLOCK <= LANES, "block-selection kernel holds at most 128 selection blocks"
    nsb = LANES
    ncp = kc4.shape[3]
    ci = np.arange(ncp)[:, None] * CMP_STRIDE
    bj = np.arange(nsb)[None, :] * SEL_BLOCK
    ovl = ((ci < bj + SEL_BLOCK) & (ci + CMP_BLOCK > bj) & (np.arange(ncp)[:, None] < ncp - 1))
    ovl = jnp.asarray(ovl, BF16)
    tri = jnp.asarray(np.tril(np.ones((nsb, nsb))), BF16)
    full = lambda a: pl.BlockSpec(a.shape, lambda b, k, i: (0,) * a.ndim)
    return pl.pallas_call(
        functools.partial(_select_kernel, nsb=nsb),
        grid=(bsz, N_KV, seq // Q_TILE),
        in_specs=[pl.BlockSpec((None, None, GROUP, Q_TILE, LANES), lambda b, k, i: (b, k, 0, i, 0)),
                  pl.BlockSpec((None, None, LANES, ncp), lambda b, k, i: (b, k, 0, 0)),
                  pl.BlockSpec((None, None, ncp, HEAD_DIM), lambda b, k, i: (b, k, 0, 0)),
                  full(ovl), full(tri)],
        out_specs=[pl.BlockSpec((None, Q_TILE, GROUP * HEAD_DIM), lambda b, k, i: (b, i, k)),
                   pl.BlockSpec((None, None, Q_TILE, nsb), lambda b, k, i: (b, k, i, 0))],
        out_shape=[jax.ShapeDtypeStruct((bsz, seq, D_NSA), F32),
                   jax.ShapeDtypeStruct((bsz, N_KV, seq, nsb), BF16)],
        compiler_params=_cparams(("parallel", "parallel", "parallel")),
        name="nsa_select",
    )(q4, kc4, vc4, ovl, tri)


def _attend_kernel(q_ref, bias_ref, ka_ref, va_ref, kw_ref, vw_ref, os_ref, ow_ref):
    s0 = pl.program_id(2) * Q_TILE
    rows = GROUP * Q_TILE
    q = q_ref[...].reshape(rows, LANES)
    bias = bias_ref[...]
    qa = jnp.concatenate([q, jnp.concatenate([bias] * GROUP, axis=0)], axis=1)
    t_row = s0 + (lax.broadcasted_iota(jnp.int32, (rows, 1), 0) & (Q_TILE - 1))

    def step(tile, width, carry, masked):
        m, acc = carry
        k0 = pl.multiple_of(tile * width, width)
        s = _dot(qa, ka_ref[:, pl.ds(k0, width)])
        if masked:
            kpos = k0 + lax.broadcasted_iota(jnp.int32, (rows, width), 1)
            s = jnp.where(kpos <= t_row, s, NEG)
        mn = jnp.maximum(m, jnp.max(s, axis=1, keepdims=True))
        p = jnp.exp(s - mn)
        acc = jnp.exp(m - mn) * acc + _dot(p.astype(BF16), va_ref[pl.ds(k0, width), :])
        return mn, acc

    n_diag = s0 // K_DIAG
    n_wide = s0 // K_TILE
    init = (jnp.full((rows, 1), NEG, F32), jnp.zeros((rows, LANES), F32))
    carry = lax.fori_loop(0, n_wide, lambda kt, c: step(kt, K_TILE, c, False), init)
    carry = lax.fori_loop(n_wide * (K_TILE // K_DIAG), n_diag,
                          lambda kt, c: step(kt, K_DIAG, c, False), carry)
    _, acc = step(n_diag, K_DIAG, carry, True)
    os_ref[...] = _heads_to_lanes(acc[:, :HEAD_DIM] / acc[:, HEAD_DIM:HEAD_DIM + 1])

    span = WINDOW + Q_TILE
    w0 = pl.multiple_of(jnp.maximum(s0 - WINDOW, 0), Q_TILE)
    s = _dot(q, kw_ref[:, pl.ds(w0, span)])
    kpos = w0 + lax.broadcasted_iota(jnp.int32, (rows, span), 1)
    mask = (kpos <= t_row) & (kpos > t_row - WINDOW)
    s = jnp.where(mask, s, NEG)
    p = jnp.where(mask, jnp.exp(s - jnp.max(s, axis=1, keepdims=True)), 0.0)
    acc = _dot(p.astype(BF16), vw_ref[pl.ds(w0, span), :])
    ow_ref[...] = _heads_to_lanes(acc[:, :HEAD_DIM] / acc[:, HEAD_DIM:HEAD_DIM + 1])


def _nsa_attend(q4, bias, ka, va, kw, vw, bsz, seq):
    per_head = lambda n: pl.BlockSpec((None, None, seq, n), lambda b, k, i: (b, k, 0, 0))
    per_head_t = lambda n: pl.BlockSpec((None, None, n, seq), lambda b, k, i: (b, k, 0, 0))
    out = pl.BlockSpec((None, Q_TILE, GROUP * HEAD_DIM), lambda b, k, i: (b, i, k))
    return pl.pallas_call(
        _attend_kernel,
        grid=(bsz, N_KV, seq // Q_TILE),
        in_specs=[pl.BlockSpec((None, None, GROUP, Q_TILE, LANES), lambda b, k, i: (b, k, 0, i, 0)),
                  pl.BlockSpec((None, None, Q_TILE, LANES), lambda b, k, i: (b, k, i, 0)),
                  per_head_t(2 * LANES), per_head(LANES), per_head_t(LANES), per_head(LANES)],
        out_specs=[out, out],
        out_shape=[jax.ShapeDtypeStruct((bsz, seq, D_NSA), F32)] * 2,
        compiler_params=_cparams(("parallel", "parallel", "arbitrary")),
        name="nsa_attend",
    )(q4, bias, ka, va, kw, vw)


def _memkv_kernel(m_ref, g_ref, wk_ref, wv_ref, k_ref, v_ref):
    mn = _rms(m_ref[...], g_ref[...]).astype(BF16)
    k_ref[...] = _dot(mn, wk_ref[...]).astype(k_ref.dtype)
    v_ref[...] = _dot(mn, wv_ref[...]).astype(v_ref.dtype)


def _mem_kv(mem, g, wk, wv):
    bsz, nm, _ = mem.shape
    g2 = g.reshape(1, D_MODEL)
    full = lambda a: pl.BlockSpec(a.shape, lambda b: (0,) * a.ndim)
    blk = pl.BlockSpec((None, nm, D_MODEL), lambda b: (b, 0, 0))
    return pl.pallas_call(
        _memkv_kernel,
        grid=(bsz,),
        in_specs=[blk, full(g2), full(wk), full(wv)],
        out_specs=[blk, blk],
        out_shape=[jax.ShapeDtypeStruct(mem.shape, BF16)] * 2,
        compiler_params=_cparams(("parallel",)),
        name="mem_kv",
    )(mem, g2, wk, wv)


def _mix_kernel(x_ref, ylru_ref, oc_ref, os_ref, ow_ref, gate_ref, e_ref, gnsa_ref, wout_ref,
                gq_ref, wq_ref, mk_ref, mv_ref, wo_ref, o_ref):
    g = gate_ref[...]
    g_hi = g.astype(BF16)
    g_lo = (g - g_hi.astype(F32)).astype(BF16)
    e = e_ref[...]
    gx = _dot(g_hi, e) + _dot(g_lo, e)
    y = (gx[:, 0:D_NSA] * oc_ref[...] + gx[:, D_NSA:2 * D_NSA] * os_ref[...]
         + gx[:, 2 * D_NSA:3 * D_NSA] * ow_ref[...])
    yn = _rms(y, gnsa_ref[...]).astype(BF16)
    ycat = jnp.concatenate([ylru_ref[...], yn], axis=1)
    x1 = x_ref[...] + _dot(ycat, wout_ref[...])
    xq = _rms(x1, gq_ref[...]).astype(BF16)
    qm = _dot(xq, wq_ref[...]).astype(BF16)
    outs = []
    for h in range(MEM_HEADS):
        sl = slice(h * MEM_HEAD_DIM, (h + 1) * MEM_HEAD_DIM)
        s = _dot(qm[:, sl], mk_ref[sl, :])
        p = jnp.exp(s - jnp.max(s, axis=1, keepdims=True))
        p = p / jnp.sum(p, axis=1, keepdims=True)
        outs.append(_dot(p.astype(BF16), mv_ref[:, sl]))
    o = jnp.concatenate(outs, axis=1).astype(BF16)
    o_ref[...] = x1 + _dot(o, wo_ref[...])


def _mix_mem(x3, ylru, oc, osl, ow, gates, g_nsa, w_out, g_q, w_mq, mk, mv, w_mo):
    bsz, seq, _ = x3.shape
    tm = 512
    e = np.zeros((LANES, 3 * D_NSA), np.float32)
    for h in range(N_HEADS):
        for j in range(3):
            e[h * 3 + j, j * D_NSA + h * HEAD_DIM:j * D_NSA + (h + 1) * HEAD_DIM] = 1.0
    e = jnp.asarray(e, BF16)
    vec = lambda a: a.reshape(1, -1)
    wq = (w_mq * (MEM_HEAD_DIM ** -0.5)).astype(BF16)
    row = lambda n: pl.BlockSpec((None, tm, n), lambda b, i: (b, i, 0))
    full = lambda a: pl.BlockSpec(a.shape, lambda b, i: (0,) * a.ndim)
    memb = pl.BlockSpec((None, mv.shape[1], D_MODEL), lambda b, i: (b, 0, 0))
    membt = pl.BlockSpec((None, D_MODEL, mv.shape[1]), lambda b, i: (b, 0, 0))
    consts = [e, vec(g_nsa), w_out.astype(BF16), vec(g_q), wq]
    return pl.pallas_call(
        _mix_kernel,
        grid=(bsz, seq // tm),
        in_specs=[row(D_MODEL), row(D_LRU), row(D_NSA), row(D_NSA), row(D_NSA), row(LANES)]
        + [full(a) for a in consts] + [membt, memb, full(w_mo)],
        out_specs=row(D_MODEL),
        out_shape=jax.ShapeDtypeStruct(x3.shape, F32),
        compiler_params=_cparams(("parallel", "parallel")),
        name="mix_mem",
    )(x3, ylru, oc, osl, ow, gates.reshape(bsz, seq, LANES), *consts, mk, mv, w_mo)


def _prefix_count(rows, test):
    pick = jnp.where
    b3 = test(rows[7])
    b2 = test(pick(b3, rows[11], rows[3]))
    b1 = test(pick(b3, pick(b2, rows[13], rows[9]), pick(b2, rows[5], rows[1])))
    b0 = test(pick(b3, pick(b2, pick(b1, rows[14], rows[12]), pick(b1, rows[10], rows[8])),
                   pick(b2, pick(b1, rows[6], rows[4]), pick(b1, rows[2], rows[0]))))
    cnt = pick(b3, 8.0, 0.0) + pick(b2, 4.0, 0.0) + pick(b1, 2.0, 0.0) + pick(b0, 1.0, 0.0)
    return pick(test(rows[15]), 16.0, cnt)


def _peer_pairs():
    k = PEER_TOPK
    return [(a, b) for a in range(k) for b in range(k) if (a + 1) * (b + 1) <= k]


def _route_kernel(x_ref, g_ref, wpq_ref, sk_ref, xn_ref, r2_ref, e2_ref, ns_ref, sc_ref,
                  q_scr, s_scr, l_scr, t_scr, *, tt):
    k = PEER_TOPK
    nhc = 2 * PEER_HEADS
    xn32 = _rms(x_ref[...], g_ref[...])
    xn = xn32.astype(BF16)
    xn_ref[...] = xn32.T.astype(BF16)
    q = _dot(xn, wpq_ref[...]).astype(BF16)
    for hc in range(nhc):
        q_scr[hc] = q[:, hc * PEER_NKEYS:(hc + 1) * PEER_NKEYS]

    def score_and_sort(hc, carry):
        st = _dot_nt(sk_ref[hc], q_scr[hc])
        s_scr[hc] = st
        rows = _topk_rows(st, want_sorted=True)
        h, c = hc // 2, hc % 2
        for a in range(k):
            l_scr[c, a, pl.ds(h, 1), :] = rows[a][0:1, :]
        return carry

    lax.fori_loop(0, nhc, score_and_sort, 0)
    v1 = [l_scr[0, a] for a in range(k)]
    v2 = [l_scr[1, b] for b in range(k)]
    pairs = _peer_pairs()
    cand = [v1[a] + v2[b] for a, b in pairs]
    pad = [jnp.full_like(cand[0], -jnp.inf)] * (4 * k - len(cand))
    groups = [_sort_desc((cand + pad)[i * k:(i + 1) * k]) for i in range(4)]
    left = _bitonic_merge_desc(_top_half(groups[0], groups[1]))
    right = _bitonic_merge_desc(_top_half(groups[2], groups[3]))
    tau = functools.reduce(jnp.minimum, _top_half(left, right))
    top = v1[0] + v2[0]
    z = jnp.zeros_like(top)
    for c_ in cand:
        z = z + jnp.where(c_ >= tau, jnp.exp(c_ - top), 0.0)
    t_scr[0] = tau
    t_scr[1] = 1.0 / z

    def per_head(h, carry):
        s1, s2 = s_scr[2 * h], s_scr[2 * h + 1]
        tau_h = t_scr[0, pl.ds(h, 1), :]
        zinv_h = t_scr[1, pl.ds(h, 1), :]
        l2 = [l_scr[1, b, pl.ds(h, 1), :] for b in range(k)]
        r2_ref[h] = _prefix_count(l2, lambda v: v > s2).astype(r2_ref.dtype)
        ns_ref[h] = _prefix_count(l2, lambda v: (s1 + v) >= tau_h)
        e2_ref[h] = jnp.exp(s2 - l_scr[1, 0, pl.ds(h, 1), :]).astype(e2_ref.dtype)
        sc_ref[h] = jnp.exp(s1 - l_scr[0, 0, pl.ds(h, 1), :]) * zinv_h
        return carry

    lax.fori_loop(0, PEER_HEADS, per_head, 0)


def _peer_route(x2d, g_ffn, w_pq, sub_keys):
    m = x2d.shape[0]
    tt = min(512, m)
    g = g_ffn.reshape(1, D_MODEL)
    wpq = w_pq.astype(BF16)
    sk = sub_keys.astype(BF16).reshape(2 * PEER_HEADS, PEER_NKEYS, PEER_QDIM // 2)
    full = lambda a: pl.BlockSpec(a.shape, lambda i: (0,) * a.ndim)
    tcol = pl.BlockSpec((PEER_HEADS, PEER_NKEYS, tt), lambda i: (0, 0, i))
    shp = lambda dt: jax.ShapeDtypeStruct((PEER_HEADS, PEER_NKEYS, m), dt)
    return pl.pallas_call(
        functools.partial(_route_kernel, tt=tt),
        grid=(m // tt,),
        in_specs=[pl.BlockSpec((tt, D_MODEL), lambda i: (i, 0)), full(g), full(wpq), full(sk)],
        out_specs=[pl.BlockSpec((D_MODEL, tt), lambda i: (0, i)), tcol, tcol, tcol, tcol],
        out_shape=[jax.ShapeDtypeStruct((D_MODEL, m), BF16), shp(BF16), shp(BF16), shp(F32), shp(F32)],
        scratch_shapes=[pltpu.VMEM((2 * PEER_HEADS, tt, PEER_NKEYS), BF16),
                        pltpu.VMEM((2 * PEER_HEADS, PEER_NKEYS, tt), F32),
                        pltpu.VMEM((2, PEER_TOPK, PEER_HEADS, tt), F32),
                        pltpu.VMEM((2, PEER_HEADS, tt), F32)],
        compiler_params=_cparams(("parallel",)),
        name="peer_route",
    )(x2d, g, wpq, sk)


def _dense_kernel(x_ref, xn_ref, r2_ref, e2_ref, ns_ref, sc_ref, u_ref, vt_ref, gf_ref,
                  o_ref, acc_ref, a_ref, *, i1b, tt):
    ei = pl.program_id(1)

    @pl.when(ei == 0)
    def _():
        acc_ref[...] = jnp.zeros_like(acc_ref)

    pack = 2 * SUBLANES
    tiles = PEER_NKEYS // pack
    view = lambda a: a.reshape(tiles, pack, tt)

    def row_tile(ref, h, i1):
        return jnp.broadcast_to(ref[h, pl.ds(i1, 1), :], (pack, tt)).astype(BF16)[None]

    gh = _gelu(_dot(u_ref[...], xn_ref[...]).astype(BF16))
    for kk in range(i1b):
        i1 = ei * i1b + kk
        w = None
        for h in range(PEER_HEADS):
            n_tile = row_tile(ns_ref, h, i1)
            s_tile = row_tile(sc_ref, h, i1)
            term = jnp.where(view(r2_ref[h]) < n_tile, view(e2_ref[h]), jnp.zeros((), BF16)) * s_tile
            w = term if w is None else w + term
        rs = slice(kk * PEER_NKEYS, (kk + 1) * PEER_NKEYS)
        a_ref[rs, :] = w.reshape(PEER_NKEYS, tt) * gh[rs, :]
    acc_ref[...] += _dot(vt_ref[...], a_ref[...])

    @pl.when(ei == pl.num_programs(1) - 1)
    def _():
        o_ref[...] = _rms(x_ref[...] + acc_ref[...].T, gf_ref[...])


def _peer_dense(x2d, xn, r2, e2, ns, sc, peer_u, peer_v, g_final):
    m = x2d.shape[0]
    tt = min(512, m)
    i1b = 16
    te = i1b * PEER_NKEYS
    u = peer_u.astype(BF16)
    vt = peer_v.T.astype(BF16)
    gf = g_final.reshape(1, D_MODEL)
    tok = pl.BlockSpec((tt, D_MODEL), lambda t, e: (t, 0))
    tcol = pl.BlockSpec((PEER_HEADS, PEER_NKEYS, tt), lambda t, e: (0, 0, t))
    return pl.pallas_call(
        functools.partial(_dense_kernel, i1b=i1b, tt=tt),
        grid=(m // tt, PEER_NKEYS // i1b),
        in_specs=[tok, pl.BlockSpec((D_MODEL, tt), lambda t, e: (0, t)), tcol, tcol, tcol, tcol,
                  pl.BlockSpec((te, D_MODEL), lambda t, e: (e, 0)),
                  pl.BlockSpec((D_MODEL, te), lambda t, e: (0, e)),
                  pl.BlockSpec((1, D_MODEL), lambda t, e: (0, 0))],
        out_specs=tok,
        out_shape=jax.ShapeDtypeStruct((m, D_MODEL), F32),
        scratch_shapes=[pltpu.VMEM((D_MODEL, tt), F32), pltpu.VMEM((te, tt), BF16)],
        compiler_params=_cparams(("parallel", "arbitrary")),
        name="peer_dense",
    )(x2d, xn, r2, e2, ns, sc, u, vt, gf)


def _per_kv_head(a, bsz, seq):
    return a.reshape(bsz, seq, N_KV, HEAD_DIM).transpose(0, 2, 1, 3)


def _layer(x, mem, p):
    bsz, seq, _ = x.shape
    m = bsz * seq
    xl, gl, q4, kvc, ka, va, kw, vw, gates = _in_proj(x, p["g_mix"], p["w_in"], p["b_gate"])
    ylru = _rg_lru(xl, gl, p["conv_w"], p["conv_b"], p["w_rg_a"], p["b_rg_a"], p["w_rg_i"],
                   p["b_rg_i"], p["lam"], p["g_out_lru"], bsz, seq)
    kc, vc = _compress(kvc, p["cmp_pos_k"], p["cmp_w1_k"], p["cmp_w2_k"], p["cmp_pos_v"],
                       p["cmp_w1_v"], p["cmp_w2_v"], bsz, seq)
    ncp = seq // CMP_STRIDE
    pad64 = lambda a: jnp.pad(a, ((0, 0),) * (a.ndim - 1) + ((0, LANES - HEAD_DIM),))
    swap = lambda a: jnp.swapaxes(a, -1, -2)
    kc4 = swap(pad64(kc.reshape(bsz, ncp, N_KV, HEAD_DIM).transpose(0, 2, 1, 3)))
    vc4 = vc.reshape(bsz, ncp, N_KV, HEAD_DIM).transpose(0, 2, 1, 3)
    o_cmp, bias = _nsa_select(q4, kc4, vc4, bsz, seq)
    o_slc, o_win = _nsa_attend(q4, bias, ka, va, kw, vw, bsz, seq)

    mk, mv = _mem_kv(mem, p["g_mem_kv"], p["w_mk"].astype(BF16), p["w_mv"].astype(BF16))
    x2 = _mix_mem(x, ylru, o_cmp, o_slc, o_win, gates, p["g_out_nsa"], p["w_out"], p["g_mem_q"],
                  p["w_mq"], swap(mk), mv, p["w_mo"].astype(BF16))
    return x2.reshape(m, D_MODEL)


def kernel(x, mem, g_mix, w_in, b_gate, conv_w, conv_b, w_rg_a, b_rg_a, w_rg_i, b_rg_i, lam,
           cmp_pos_k, cmp_w1_k, cmp_w2_k, cmp_pos_v, cmp_w1_v, cmp_w2_v, g_out_lru, g_out_nsa,
           w_out, g_mem_q, g_mem_kv, w_mq, w_mk, w_mv, w_mo, g_ffn, w_pq, sub_keys, peer_u,
           peer_v, g_final):
    depth = w_in.shape[0]
    assert depth == 1, "the final norm is fused into the single layer's PEER kernel"
    names = ["g_mix", "w_in", "b_gate", "conv_w", "conv_b", "w_rg_a", "b_rg_a", "w_rg_i", "b_rg_i",
             "lam", "cmp_pos_k", "cmp_w1_k", "cmp_w2_k", "cmp_pos_v", "cmp_w1_v", "cmp_w2_v",
             "g_out_lru", "g_out_nsa", "w_out", "g_mem_q", "g_mem_kv", "w_mq", "w_mk", "w_mv", "w_mo",
             "g_ffn", "w_pq", "sub_keys", "peer_u", "peer_v"]
    vals = [g_mix, w_in, b_gate, conv_w, conv_b, w_rg_a, b_rg_a, w_rg_i, b_rg_i, lam, cmp_pos_k,
            cmp_w1_k, cmp_w2_k, cmp_pos_v, cmp_w1_v, cmp_w2_v, g_out_lru, g_out_nsa, w_out, g_mem_q,
            g_mem_kv, w_mq, w_mk, w_mv, w_mo, g_ffn, w_pq, sub_keys, peer_u, peer_v]
    p = {n: v[0] for n, v in zip(names, vals)}
    bsz, seq, _ = x.shape
    x2 = _layer(x, mem, p)
    xn, r2, e2, ns, sc = _peer_route(x2, p["g_ffn"], p["w_pq"], p["sub_keys"])
    out = _peer_dense(x2, xn, r2, e2, ns, sc, p["peer_u"], p["peer_v"], g_final)
    return out.reshape(bsz, seq, D_MODEL)
```
